```python
import math
import jax, jax.numpy as jnp
from jax import lax
import numpy as np

D_MODEL = 1024
BATCH = 32
SEQ = 2048
DEPTH = 1
DEC_BATCH = 128
DEC_SEQ = 1
PAST_LEN = 8192
PAGE_SIZE = 128

N_HEADS_A = 8
HEAD_DIM_A = 64
D_ATTN = N_HEADS_A * HEAD_DIM_A
DILATED_PATTERNS = ((128, 1), (512, 4), (2048, 16))
MAX_WINDOW = 2048
NUM_BUCKETS = 32
MAX_DISTANCE = 2048
N_HEADS_B = 16
HEAD_DIM_B = 64
D_INNER = N_HEADS_B * HEAD_DIM_B
N_SSM_GROUPS = 2
HEADS_PER_GROUP = N_HEADS_B // N_SSM_GROUPS
D_STATE = 128
CONV_WIDTH = 4
SSD_CHUNK = 128
D_XBC = D_INNER + 2 * N_SSM_GROUPS * D_STATE
D_MIX = D_ATTN + D_INNER
D_IN_PROJ = 3 * D_ATTN + D_INNER + D_XBC + N_HEADS_B
N_MEM = 256
N_HEADS_X = 4
HEAD_DIM_X = D_MODEL // N_HEADS_X
D_FF = ((8 * D_MODEL // 3 + 127) // 128) * 128
EPS = 1e-6
NEG_INF = -1e30

kernel_name = "hymba_dilated_ssd_macaron_step"


def rms_norm(x, g):
    x32 = x.astype(jnp.float32)
    y = x32 * lax.rsqrt(jnp.mean(x32 * x32, axis=-1, keepdims=True) + EPS)
    return (y * g.astype(jnp.float32)).astype(x.dtype)


def swiglu(x, w_gate, w_up, w_down):
    return (jax.nn.silu(x @ w_gate) * (x @ w_up)) @ w_down


def t5_bucket(dist):
    max_exact = NUM_BUCKETS // 2
    d = jnp.maximum(dist, 0)
    df = jnp.maximum(d, 1).astype(jnp.float32)
    large = max_exact + (jnp.log(df / max_exact) / math.log(MAX_DISTANCE / max_exact)
                         * (NUM_BUCKETS - max_exact)).astype(jnp.int32)
    large = jnp.minimum(large, NUM_BUCKETS - 1)
    return jnp.where(d < max_exact, d, large)


def dilated_attn_prompt(q, k, v, rel_bias, window, dilation):
    b, s, h, dh = q.shape
    n_back = window // dilation
    blk = n_back
    ls = s // dilation
    nb = -(-ls // blk)
    lp = nb * blk
    n = b * dilation

    def by_residue(a):
        a = a.reshape(b, ls, dilation, h, dh).transpose(0, 2, 1, 3, 4).reshape(n, ls, h, dh)
        a = jnp.pad(a, ((0, 0), (0, lp - ls), (0, 0), (0, 0)))
        return a.reshape(n, nb, blk, h, dh)

    def with_prev_block(a):
        prev = jnp.concatenate([jnp.zeros_like(a[:, :1]), a[:, :-1]], axis=1)
        return jnp.concatenate([prev, a], axis=2)

    qb = by_residue(q)
    kb = with_prev_block(by_residue(k))
    vb = with_prev_block(by_residue(v))
    qi = jnp.arange(blk)[:, None]
    kj = jnp.arange(2 * blk)[None, :]
    delta = qi + blk - kj
    band = (delta >= 0) & (delta <= n_back)
    exists = (jnp.arange(nb)[:, None, None] > 0) | (kj[None] >= blk)
    mask = band[None] & exists
    bias = jnp.transpose(rel_bias[t5_bucket(delta * dilation)], (2, 0, 1)).astype(jnp.float32)
    scores = jnp.einsum('nbqhd,nbkhd->nbhqk', qb, kb).astype(jnp.float32) * (HEAD_DIM_A ** -0.5)
    scores = jnp.where(mask[None, :, None], scores + bias[None, None], NEG_INF)
    lse = jax.nn.logsumexp(scores, axis=-1)
    p = jnp.exp(scores - lse[..., None])
    o = jnp.einsum('nbhqk,nbkhd->nbqhd', p, vb.astype(jnp.float32))
    o = o.reshape(n, lp, h, dh)[:, :ls]
    lse = jnp.transpose(lse, (0, 1, 3, 2)).reshape(n, lp, h)[:, :ls]
    o = o.reshape(b, dilation, ls, h, dh).transpose(0, 2, 1, 3, 4).reshape(b, s, h, dh)
    lse = lse.reshape(b, dilation, ls, h).transpose(0, 2, 1, 3).reshape(b, s, h)
    return o, lse


def dilated_attn_sample(q, k_all, v_all, rel_bias, window, dilation):
    t = q.shape[1]
    w = k_all.shape[1] - t
    offs = jnp.arange(window // dilation + 1) * dilation
    idx = w + jnp.arange(t)[:, None] - offs[None, :]
    valid = idx >= 0
    idx = jnp.maximum(idx, 0)
    kg = k_all[:, idx]
    vg = v_all[:, idx]
    bias = jnp.transpose(rel_bias[t5_bucket(offs)], (1, 0)).astype(jnp.float32)
    scores = jnp.einsum('bthd,btkhd->bhtk', q, kg).astype(jnp.float32) * (HEAD_DIM_A ** -0.5)
    scores = jnp.where(valid[None, None], scores + bias[None, :, None, :], NEG_INF)
    lse = jax.nn.logsumexp(scores, axis=-1)
    p = jnp.exp(scores - lse[..., None])
    o = jnp.einsum('bhtk,btkhd->bthd', p, vg.astype(jnp.float32))
    return o, jnp.transpose(lse, (0, 2, 1))


def denominator_mixture(outs, lses):
    wts = jax.nn.softmax(jnp.stack(lses, 0), axis=0)
    return jnp.einsum('pblh,pblhd->blhd', wts, jnp.stack(outs, 0))


def ssd_chunked(xs, dt, a, bm, cm, h0):
    b, L, G, E, P = xs.shape
    N = bm.shape[-1]
    Q = min(SSD_CHUNK, L)
    lp = -(-L // Q) * Q
    pad = lp - L
    xs = xs.astype(jnp.float32)
    bm = bm.astype(jnp.float32)
    cm = cm.astype(jnp.float32)
    if pad:
        xs = jnp.pad(xs, ((0, 0), (0, pad), (0, 0), (0, 0), (0, 0)))
        dt = jnp.pad(dt, ((0, 0), (0, pad), (0, 0), (0, 0)))
        bm = jnp.pad(bm, ((0, 0), (0, pad), (0, 0), (0, 0)))
        cm = jnp.pad(cm, ((0, 0), (0, pad), (0, 0), (0, 0)))
    nc = lp // Q
    xdt = (xs * dt[..., None]).reshape(b, nc, Q, G, E, P)
    la = (dt * a[None, None]).reshape(b, nc, Q, G, E).transpose(0, 1, 3, 4, 2)
    a_cs = jnp.cumsum(la, axis=-1)
    bc = bm.reshape(b, nc, Q, G, N)
    cc = cm.reshape(b, nc, Q, G, N)
    causal = jnp.tril(jnp.ones((Q, Q), dtype=bool))
    seg = a_cs[..., :, None] - a_cs[..., None, :]
    lmat = jnp.exp(jnp.where(causal, seg, -jnp.inf))
    cb = jnp.einsum('bclgn,bcsgn->bcgls', cc, bc)
    y_diag = jnp.einsum('bcgls,bcgels,bcsgep->bclgep', cb, lmat, xdt)
    decay_s = jnp.exp(a_cs[..., -1:] - a_cs)
    st = jnp.einsum('bcsgn,bcges,bcsgep->bcgepn', bc, decay_s, xdt)
    chunk_decay = jnp.exp(a_cs[..., -1])

    def step(h, inp):
        st_c, dec_c = inp
        return dec_c[..., None, None] * h + st_c, h

    h_last, h_in = lax.scan(step, h0, (jnp.transpose(st, (1, 0, 2, 3, 4, 5)),
                                       jnp.transpose(chunk_decay, (1, 0, 2, 3))))
    h_in = jnp.transpose(h_in, (1, 0, 2, 3, 4, 5))
    y_off = jnp.einsum('bclgn,bcgepn,bcgel->bclgep', cc, h_in, jnp.exp(a_cs))
    y = (y_diag + y_off).reshape(b, lp, G, E, P)[:, :L]
    return y, h_last


def mamba2_mixer(z, xbc_raw, dt_raw, conv_prev, ssm_prev, conv_w, conv_b, dt_bias, a_log, d_skip, g_ssm):
    b, l, _ = z.shape
    xin = jnp.concatenate([conv_prev.astype(jnp.float32), xbc_raw.astype(jnp.float32)], axis=1)
    new_conv = xin[:, -(CONV_WIDTH - 1):].astype(xbc_raw.dtype)
    xbc = lax.conv_general_dilated(xin, conv_w.astype(jnp.float32)[:, None, :], window_strides=(1,),
                                   padding='VALID', dimension_numbers=('NWC', 'WIO', 'NWC'),
                                   feature_group_count=D_XBC)
    xbc = jax.nn.silu(xbc + conv_b.astype(jnp.float32))
    xs = xbc[..., :D_INNER].reshape(b, l, N_SSM_GROUPS, HEADS_PER_GROUP, HEAD_DIM_B)
    bm = xbc[..., D_INNER:D_INNER + N_SSM_GROUPS * D_STATE].reshape(b, l, N_SSM_GROUPS, D_STATE)
    cm = xbc[..., D_INNER + N_SSM_GROUPS * D_STATE:].reshape(b, l, N_SSM_GROUPS, D_STATE)
    dt = jax.nn.softplus(dt_raw.astype(jnp.float32) + dt_bias.astype(jnp.float32))
    dt = dt.reshape(b, l, N_SSM_GROUPS, HEADS_PER_GROUP)
    a = -jnp.exp(a_log.astype(jnp.float32)).reshape(N_SSM_GROUPS, HEADS_PER_GROUP)
    h0 = ssm_prev.astype(jnp.float32).reshape(b, N_SSM_GROUPS, HEADS_PER_GROUP, HEAD_DIM_B, D_STATE)
    y, h_last = ssd_chunked(xs, dt, a, bm, cm, h0)
    y = y + d_skip.astype(jnp.float32).reshape(N_SSM_GROUPS, HEADS_PER_GROUP)[..., None] * xs
    gate = jax.nn.silu(z.astype(jnp.float32)).reshape(b, l, N_SSM_GROUPS, D_INNER // N_SSM_GROUPS)
    yg = y.reshape(b, l, N_SSM_GROUPS, D_INNER // N_SSM_GROUPS) * gate
    yg = yg * lax.rsqrt(jnp.mean(yg * yg, axis=-1, keepdims=True) + EPS)
    y = yg.reshape(b, l, D_INNER) * g_ssm.astype(jnp.float32)
    return y, new_conv, h_last.reshape(b, N_HEADS_B, HEAD_DIM_B, D_STATE)


def memory_kv(mem, g_mem, w_ck, w_cv):
    b = mem.shape[0]
    m = rms_norm(mem, g_mem)
    mk = (m @ w_ck).reshape(b, N_MEM, N_HEADS_X, HEAD_DIM_X)
    mv = (m @ w_cv).reshape(b, N_MEM, N_HEADS_X, HEAD_DIM_X)
    return mk, mv


def memory_attend(h, mk, mv, w_cq, w_co):
    b, l, _ = h.shape
    q = (h @ w_cq).reshape(b, l, N_HEADS_X, HEAD_DIM_X)
    s = jnp.einsum('blhd,bmhd->bhlm', q, mk).astype(jnp.float32) * (HEAD_DIM_X ** -0.5)
    p = jax.nn.softmax(s, axis=-1)
    o = jnp.einsum('bhlm,bmhd->blhd', p, mv.astype(jnp.float32))
    return o.reshape(b, l, N_HEADS_X * HEAD_DIM_X).astype(h.dtype) @ w_co


def decoder_layer(x, k_prev, v_prev, conv_prev, ssm_prev, mem_k, mem_v, rel_bias,
                  g_ffn1, w1_gate, w1_up, w1_down, g_mix, w_in, conv_w, conv_b, dt_bias, a_log,
                  d_skip, g_ssm, w_out, g_cross, w_cq, w_co, g_ffn2, w2_gate, w2_up, w2_down):
    b, l, _ = x.shape
    x = x + (0.5 * swiglu(rms_norm(x, g_ffn1), w1_gate, w1_up, w1_down)).astype(x.dtype)
    h = rms_norm(x, g_mix)
    proj = h @ w_in
    cuts = [D_ATTN, 2 * D_ATTN, 3 * D_ATTN, 3 * D_ATTN + D_INNER, 3 * D_ATTN + D_INNER + D_XBC]
    q, k, v, z, xbc_raw, dt_raw = jnp.split(proj, cuts, axis=-1)
    q = q.reshape(b, l, N_HEADS_A, HEAD_DIM_A)
    k = k.reshape(b, l, N_HEADS_A, HEAD_DIM_A)
    v = v.reshape(b, l, N_HEADS_A, HEAD_DIM_A)
    if k_prev is None:
        res = [dilated_attn_prompt(q, k, v, rel_bias, wnd, dil) for (wnd, dil) in DILATED_PATTERNS]
        keep = min(MAX_WINDOW, l)
        new_k, new_v = k[:, l - keep:], v[:, l - keep:]
        conv_prev = jnp.zeros((b, CONV_WIDTH - 1, D_XBC), xbc_raw.dtype)
        ssm_prev = jnp.zeros((b, N_HEADS_B, HEAD_DIM_B, D_STATE), jnp.float32)
    else:
        k_all = jnp.concatenate([k_prev.astype(k.dtype), k], axis=1)
        v_all = jnp.concatenate([v_prev.astype(v.dtype), v], axis=1)
        res = [dilated_attn_sample(q, k_all, v_all, rel_bias, wnd, dil) for (wnd, dil) in DILATED_PATTERNS]
        new_k, new_v = k, v
    o_attn = denominator_mixture([r[0] for r in res], [r[1] for r in res]).reshape(b, l, D_ATTN)
    y_ssm, new_conv, new_ssm = mamba2_mixer(z, xbc_raw, dt_raw, conv_prev, ssm_prev, conv_w, conv_b,
                                            dt_bias, a_log, d_skip, g_ssm)
    mixed = jnp.concatenate([o_attn.astype(x.dtype), y_ssm.astype(x.dtype)], axis=-1)
    x = x + (mixed @ w_out).astype(x.dtype)
    x = x + memory_attend(rms_norm(x, g_cross), mem_k, mem_v, w_cq, w_co).astype(x.dtype)
    x = x + (0.5 * swiglu(rms_norm(x, g_ffn2), w2_gate, w2_up, w2_down)).astype(x.dtype)
    return x, new_k, new_v, new_conv, new_ssm


def setup_inputs(seed: int = 0) -> dict:
    key = jax.random.key(seed)
    ks = iter(jax.random.split(key, 48))

    def nrm(shape, scale=1.0):
        return jax.random.normal(next(ks), shape, jnp.float32) * scale

    def gain(shape):
        return 1.0 + nrm(shape, 0.05)

    w_buf = min(MAX_WINDOW, PAST_LEN)
    L = DEPTH
    inp = {}
    inp["x_prompt"] = nrm((BATCH, SEQ, D_MODEL))
    inp["x_sample"] = nrm((DEC_BATCH, DEC_SEQ, D_MODEL))
    inp["cache_win_k"] = nrm((L, DEC_BATCH, w_buf, N_HEADS_A, HEAD_DIM_A))
    inp["cache_win_v"] = nrm((L, DEC_BATCH, w_buf, N_HEADS_A, HEAD_DIM_A))
    inp["cache_conv"] = nrm((L, DEC_BATCH, CONV_WIDTH - 1, D_XBC))
    inp["state_ssm"] = nrm((L, DEC_BATCH, N_HEADS_B, HEAD_DIM_B, D_STATE), 0.1)
    inp["cache_mem_k"] = nrm((L, DEC_BATCH, N_MEM, N_HEADS_X, HEAD_DIM_X))
    inp["cache_mem_v"] = nrm((L, DEC_BATCH, N_MEM, N_HEADS_X, HEAD_DIM_X))
    inp["mem_prompt"] = nrm((BATCH, N_MEM, D_MODEL))
    inp["rel_bias"] = nrm((NUM_BUCKETS, N_HEADS_A), 0.5)
    inp["g_ffn1"] = gain((L, D_MODEL))
    inp["w1_gate"] = nrm((L, D_MODEL, D_FF), D_MODEL ** -0.5)
    inp["w1_up"] = nrm((L, D_MODEL, D_FF), D_MODEL ** -0.5)
    inp["w1_down"] = nrm((L, D_FF, D_MODEL), D_FF ** -0.5)
    inp["g_mix"] = gain((L, D_MODEL))
    inp["w_in"] = nrm((L, D_MODEL, D_IN_PROJ), D_MODEL ** -0.5)
    inp["conv_w"] = nrm((L, CONV_WIDTH, D_XBC), CONV_WIDTH ** -0.5)
    inp["conv_b"] = nrm((L, D_XBC), 0.02)
    u = jax.random.uniform(next(ks), (L, N_HEADS_B), jnp.float32)
    dt0 = jnp.exp(u * (math.log(0.1) - math.log(0.001)) + math.log(0.001))
    inp["dt_bias"] = dt0 + jnp.log(-jnp.expm1(-dt0))
    inp["a_log"] = jnp.log(jax.random.uniform(next(ks), (L, N_HEADS_B), jnp.float32, 1.0, 16.0))
    inp["d_skip"] = gain((L, N_HEADS_B))
    inp["g_ssm"] = gain((L, D_INNER))
    inp["w_out"] = nrm((L, D_MIX, D_MODEL), D_MIX ** -0.5)
    inp["g_mem"] = gain((L, D_MODEL))
    inp["w_ck"] = nrm((L, D_MODEL, N_HEADS_X * HEAD_DIM_X), D_MODEL ** -0.5)
    inp["w_cv"] = nrm((L, D_MODEL, N_HEADS_X * HEAD_DIM_X), D_MODEL ** -0.5)
    inp["g_cross"] = gain((L, D_MODEL))
    inp["w_cq"] = nrm((L, D_MODEL, N_HEADS_X * HEAD_DIM_X), D_MODEL ** -0.5)
    inp["w_co"] = nrm((L, N_HEADS_X * HEAD_DIM_X, D_MODEL), (N_HEADS_X * HEAD_DIM_X) ** -0.5)
    inp["g_ffn2"] = gain((L, D_MODEL))
    inp["w2_gate"] = nrm((L, D_MODEL, D_FF), D_MODEL ** -0.5)
    inp["w2_up"] = nrm((L, D_MODEL, D_FF), D_MODEL ** -0.5)
    inp["w2_down"] = nrm((L, D_FF, D_MODEL), D_FF ** -0.5)
    inp["g_final"] = gain((D_MODEL,))
    return inp


def reference(x_prompt, x_sample, cache_win_k, cache_win_v, cache_conv, state_ssm, cache_mem_k,
              cache_mem_v, mem_prompt, rel_bias, g_ffn1, w1_gate, w1_up, w1_down, g_mix, w_in,
              conv_w, conv_b, dt_bias, a_log, d_skip, g_ssm, w_out, g_mem, w_ck, w_cv, g_cross,
              w_cq, w_co, g_ffn2, w2_gate, w2_up, w2_down, g_final):
    yp, ys = x_prompt, x_sample
    pk, pv, pc, pss, pmk, pmv = [], [], [], [], [], []
    sk, sv, sc, sss = [], [], [], []
    for i in range(DEPTH):
        lw = (g_ffn1[i], w1_gate[i], w1_up[i], w1_down[i], g_mix[i], w_in[i], conv_w[i], conv_b[i],
              dt_bias[i], a_log[i], d_skip[i], g_ssm[i], w_out[i], g_cross[i], w_cq[i], w_co[i],
              g_ffn2[i], w2_gate[i], w2_up[i], w2_down[i])
        mk_p, mv_p = memory_kv(mem_prompt, g_mem[i], w_ck[i], w_cv[i])
        yp, k_p, v_p, c_p, s_p = decoder_layer(yp, None, None, None, None, mk_p, mv_p, rel_bias, *lw)
        ys, k_s, v_s, c_s, s_s = decoder_layer(ys, cache_win_k[i], cache_win_v[i], cache_conv[i],
                                               state_ssm[i], cache_mem_k[i], cache_mem_v[i], rel_bias, *lw)
        pk.append(k_p); pv.append(v_p); pc.append(c_p); pss.append(s_p); pmk.append(mk_p); pmv.append(mv_p)
        sk.append(k_s); sv.append(v_s); sc.append(c_s); sss.append(s_s)
    y_prompt = rms_norm(yp, g_final)
    y_sample = rms_norm(ys, g_final)
    return (y_prompt, y_sample, jnp.stack(pk), jnp.stack(pv), jnp.stack(pc), jnp.stack(pss),
            jnp.stack(pmk), jnp.stack(pmv), jnp.stack(sk), jnp.stack(sv), jnp.stack(sc), jnp.stack(sss))
```

```python
import functools
import math

import jax
import jax.numpy as jnp
from jax import lax
from jax.experimental import pallas as pl
from jax.experimental.pallas import tpu as pltpu

F32 = jnp.float32
BF16 = jnp.bfloat16

EPS = 1e-6
NEG_INF = -1e30
DILATIONS = (1, 4, 16)
N_BACK = 128
MAX_DISTANCE = 2048
SSD_CHUNK = 128
CONV_WIDTH = 4

LANES = 128
SUBLANES = 8
VMEM_LIMIT_BYTES = 56 * 1024 * 1024

_HI = lax.Precision.HIGHEST


def _cparams(*sem):
    return pltpu.CompilerParams(dimension_semantics=sem, vmem_limit_bytes=VMEM_LIMIT_BYTES)


def _const_spec(shape):
    nd = len(shape)
    return pl.BlockSpec(shape, lambda *_: (0,) * nd, pipeline_mode=pl.Buffered(1))


def _rms(x, g):
    return x * lax.rsqrt(jnp.mean(x * x, axis=-1, keepdims=True) + EPS) * g


def _silu(x):
    return x * (1.0 / (1.0 + jnp.exp(-x)))


def _softplus(x):
    return jnp.maximum(x, 0.0) + jnp.log1p(jnp.exp(-jnp.abs(x)))


def _dot(a, b):
    return jnp.dot(a, b, preferred_element_type=F32)


def _dot_nt(a, b):
    return lax.dot_general(a, b, (((1,), (1,)), ((), ())), preferred_element_type=F32)


def _dot_tn(a, b, precision=None):
    return lax.dot_general(a, b, (((0,), (0,)), ((), ())), preferred_element_type=F32,
                           precision=precision)


def _ffn_body(x_ref, g_ref, wg_ref, wu_ref, wd_ref, gf_ref, o_ref, *, final_norm):
    x = x_ref[...]
    xn = _rms(x, g_ref[...]).astype(BF16)
    gate = _dot(xn, wg_ref[...])
    up = _dot(xn, wu_ref[...])
    h = (_silu(gate) * up).astype(BF16)
    y = x + 0.5 * _dot(h, wd_ref[...])
    if final_norm:
        y = _rms(y, gf_ref[...])
    o_ref[...] = y


def _ffn(x, g, wg, wu, wd, g_final, *, final_norm, tm):
    n, d = x.shape
    ff = wg.shape[1]
    row = pl.BlockSpec((tm, d), lambda i: (i, 0))
    return pl.pallas_call(
        functools.partial(_ffn_body, final_norm=final_norm),
        grid=(n // tm,),
        in_specs=[row, _const_spec((1, d)), _const_spec((d, ff)), _const_spec((d, ff)),
                  _const_spec((ff, d)), _const_spec((1, d))],
        out_specs=row,
        out_shape=jax.ShapeDtypeStruct((n, d), F32),
        compiler_params=_cparams("parallel"),
        name="ffn",
    )(x, g, wg, wu, wd, g_final)


def _inproj_body(x_ref, g_ref, w_ref, q_ref, k_ref, v_ref, z_ref, xbc_ref, dt_ref, *, cuts, q_scale):
    xn = _rms(x_ref[...], g_ref[...]).astype(BF16)
    proj = _dot(xn, w_ref[...])
    c0, c1, c2, c3, c4 = cuts
    q_ref[...] = proj[:, :c0] * q_scale
    k_ref[...] = proj[:, c0:c1]
    v_ref[...] = proj[:, c1:c2]
    z_ref[...] = proj[:, c2:c3]
    xbc_ref[...] = proj[:, c3:c4]
    dt_ref[...] = proj[:, c4:]


def _inproj(x, g, w_pad, *, cuts, q_scale, tm):
    n, d = x.shape
    wp = w_pad.shape[1]
    c0, c1, c2, c3, c4 = cuts
    widths = (c0, c1 - c0, c2 - c1, c3 - c2, c4 - c3, wp - c4)
    row = lambda w: pl.BlockSpec((tm, w), lambda i: (i, 0))
    return pl.pallas_call(
        functools.partial(_inproj_body, cuts=cuts, q_scale=q_scale),
        grid=(n // tm,),
        in_specs=[row(d), _const_spec((1, d)), _const_spec((d, wp))],
        out_specs=[row(w) for w in widths],
        out_shape=[jax.ShapeDtypeStruct((n, w), F32) for w in widths],
        compiler_params=_cparams("parallel"),
        name="inproj",
    )(x, g, w_pad)


def _t5_bucket(dist, num_buckets):
    max_exact = num_buckets // 2
    d = jnp.maximum(dist, 0)
    df = jnp.maximum(d, 1).astype(F32)
    large = max_exact + (jnp.log(df / max_exact) / math.log(MAX_DISTANCE / max_exact)
                         * (num_buckets - max_exact)).astype(jnp.int32)
    large = jnp.minimum(large, num_buckets - 1)
    return jnp.where(d < max_exact, d, large)


def _bias_body(rb_ref, idxp_ref, idxs_ref, bp_ref, bs_ref, *, n_buckets, n_heads):
    n_pat = idxp_ref.shape[0]
    blk = idxp_ref.shape[1]
    qi = lax.broadcasted_iota(jnp.int32, (blk, 2 * blk), 0)
    kj = lax.broadcasted_iota(jnp.int32, (blk, 2 * blk), 1)
    delta = qi + blk - kj
    band = (delta >= 0) & (delta <= N_BACK)
    for p in range(n_pat):
        idx = idxp_ref[p]
        idx_s = idxs_ref[p]
        for h in range(n_heads):
            acc = jnp.zeros(idx.shape, F32)
            acc_s = jnp.zeros(idx_s.shape, F32)
            for c in range(n_buckets):
                val = rb_ref[c, h]
                acc = jnp.where(idx == c, val, acc)
                acc_s = jnp.where(idx_s == c, val, acc_s)
            bp_ref[p, h] = jnp.where(band, acc, NEG_INF)
            bs_ref[p, h:h + 1, :] = acc_s


def _bias_tables(rel_bias):
    n_buckets, n_heads = rel_bias.shape
    blk = N_BACK
    qi = jnp.arange(blk)[:, None]
    kj = jnp.arange(2 * blk)[None, :]
    delta = qi + blk - kj
    idx_p = jnp.stack([_t5_bucket(delta * d, n_buckets) for d in DILATIONS]).astype(jnp.int32)
    steps = jnp.concatenate([blk - jnp.arange(blk), jnp.zeros((LANES,), jnp.int32)])
    idx_s = jnp.stack([_t5_bucket(steps * d, n_buckets) for d in DILATIONS]).astype(jnp.int32)
    idx_s = idx_s[:, None, :]
    n_pat = len(DILATIONS)
    return pl.pallas_call(
        functools.partial(_bias_body, n_buckets=n_buckets, n_heads=n_heads),
        in_specs=[pl.BlockSpec(memory_space=pltpu.SMEM),
                  pl.BlockSpec(memory_space=pltpu.VMEM), pl.BlockSpec(memory_space=pltpu.VMEM)],
        out_specs=[pl.BlockSpec(memory_space=pltpu.VMEM), pl.BlockSpec(memory_space=pltpu.VMEM)],
        out_shape=[jax.ShapeDtypeStruct((n_pat, n_heads, blk, 2 * blk), F32),
                   jax.ShapeDtypeStruct((n_pat, n_heads, blk + LANES), F32)],
        name="bias_tables",
    )(rel_bias, idx_p, idx_s)


def _attn_prompt_body(q_ref, k_ref, v_ref, bias_ref, o_ref, acc2, l2, m2, acc3, l3, m3, *, hd):
    seq = q_ref.shape[0]
    blk = N_BACK
    lane = lax.broadcasted_iota(jnp.int32, (1, LANES), 1)
    head_mask = (lane < hd, lane >= hd)

    def block(qb, kb, vb, bias_of_head):
        kb16 = kb.astype(BF16)
        out = None
        ms = []
        for h in range(2):
            qh = jnp.where(head_mask[h], qb, 0.0).astype(BF16)
            s = _dot_nt(qh, kb16) + bias_of_head(h)
            m = jnp.max(s, axis=-1, keepdims=True)
            p = jnp.exp(s - m).astype(BF16)
            vh = jnp.where(head_mask[h], vb, 0.0).astype(BF16)
            ones_h = jnp.broadcast_to(jnp.where(head_mask[h], 1.0, 0.0).astype(BF16), vh.shape)
            oh = _dot(p, jnp.concatenate([vh, ones_h], axis=1))
            out = oh if out is None else out + oh
            ms.append(m)
        m_full = jnp.where(head_mask[0], ms[0], ms[1])
        return out[:, :LANES], out[:, LANES:], m_full

    def cur_bias(p):
        return lambda h: bias_ref[p, h, :, blk:]

    def full_bias(p):
        return lambda h: bias_ref[p, h]

    d3 = DILATIONS[2]

    def pat3(r, carry):
        rows = pl.ds(r, blk, stride=d3)
        a, l, m = block(q_ref[rows, :], k_ref[rows, :], v_ref[rows, :], cur_bias(2))
        acc3[rows, :] = a
        l3[rows, :] = l
        m3[rows, :] = m
        return carry

    lax.fori_loop(0, d3, pat3, 0)

    d2 = DILATIONS[1]
    nb2 = seq // (d2 * blk)

    def pat2(r, carry):
        for i in range(nb2):
            rows = pl.ds(r + d2 * blk * i, blk, stride=d2)
            if i == 0:
                krows, bias = rows, cur_bias(1)
            else:
                krows, bias = pl.ds(r + d2 * blk * (i - 1), 2 * blk, stride=d2), full_bias(1)
            a, l, m = block(q_ref[rows, :], k_ref[krows, :], v_ref[krows, :], bias)
            acc2[rows, :] = a
            l2[rows, :] = l
            m2[rows, :] = m
        return carry

    lax.fori_loop(0, d2, pat2, 0)

    def finish(rows, a1, l1, m1):
        a2_, l2_, m2_ = acc2[rows, :], l2[rows, :], m2[rows, :]
        a3_, l3_, m3_ = acc3[rows, :], l3[rows, :], m3[rows, :]
        mx = jnp.maximum(jnp.maximum(m1, m2_), m3_)
        w1, w2, w3 = jnp.exp(m1 - mx), jnp.exp(m2_ - mx), jnp.exp(m3_ - mx)
        num = w1 * a1 + w2 * a2_ + w3 * a3_
        den = w1 * l1 + w2 * l2_ + w3 * l3_
        o_ref[rows, :] = (num / den).astype(o_ref.dtype)

    rows0 = pl.ds(0, blk)
    finish(rows0, *block(q_ref[rows0, :], k_ref[rows0, :], v_ref[rows0, :], cur_bias(0)))

    def pat1(i, carry):
        start = pl.multiple_of(i * blk, blk)
        rows = pl.ds(start, blk)
        krows = pl.ds(pl.multiple_of(start - blk, blk), 2 * blk)
        finish(rows, *block(q_ref[rows, :], k_ref[krows, :], v_ref[krows, :], full_bias(0)))
        return carry

    lax.fori_loop(1, seq // blk, pat1, 0)


def _attn_prompt(q, k, v, bias_p, *, hd):
    b, s, da = q.shape
    n_pairs = da // LANES
    n_pat = bias_p.shape[0]
    blk = N_BACK
    tok = pl.BlockSpec((None, s, LANES), lambda i, j: (i, 0, j))
    return pl.pallas_call(
        functools.partial(_attn_prompt_body, hd=hd),
        grid=(b, n_pairs),
        in_specs=[tok, tok, tok,
                  pl.BlockSpec((n_pat, 2, blk, 2 * blk), lambda i, j: (0, j, 0, 0))],
        out_specs=tok,
        out_shape=jax.ShapeDtypeStruct((b, s, da), BF16),
        scratch_shapes=[pltpu.VMEM((s, LANES), F32) for _ in range(6)],
        compiler_params=_cparams("parallel", "arbitrary"),
        name="attn_prompt",
    )(q, k, v, bias_p)


def _ssd_prompt_body(z_ref, xbc_ref, dt_ref, cw_ref, cb_ref, dtb_ref, alog_ref, dskip_ref, gssm_ref,
                     y_ref, hout_ref, tail_ref, ht_ref, *, n_heads, hd, n_groups, d_state):
    step = pl.program_id(1)
    ts = z_ref.shape[0]
    q = SSD_CHUNK
    d_inner = n_heads * hd
    gw = d_inner // n_groups
    pairs_per_group = gw // LANES
    n_pairs = d_inner // LANES

    @pl.when(step == 0)
    def _():
        tail_ref[...] = jnp.zeros_like(tail_ref)
        ht_ref[...] = jnp.zeros_like(ht_ref)

    lane = lax.broadcasted_iota(jnp.int32, (1, LANES), 1)
    head_mask = (lane < hd, lane >= hd)
    row_q = lax.broadcasted_iota(jnp.int32, (q, q), 0)
    col_q = lax.broadcasted_iota(jnp.int32, (q, q), 1)
    causal = row_q >= col_q
    tri = jnp.where(causal, 1.0, 0.0).astype(F32)
    tri_t = jnp.where(row_q <= col_q, 1.0, 0.0).astype(F32)
    row8 = lax.broadcasted_iota(jnp.int32, (SUBLANES, 1), 0)
    e_row = lax.broadcasted_iota(jnp.int32, (LANES, d_inner), 0)
    e_col = lax.broadcasted_iota(jnp.int32, (LANES, d_inner), 1)
    exp_mat = jnp.where((e_col >= e_row * hd) & (e_col < (e_row + 1) * hd), 1.0, 0.0).astype(F32)

    cw = cw_ref[...]
    neg_a = -jnp.exp(alog_ref[...])

    def chunk(c, carry):
        t0 = pl.multiple_of(c * q, q)
        rows = pl.ds(t0, q)
        x_raw = xbc_ref[rows, :]
        prev8 = tail_ref[...]
        pre = x_raw * cw[CONV_WIDTH - 1:CONV_WIDTH, :] + cb_ref[...]
        for k in range(1, CONV_WIDTH):
            xs_k = pltpu.roll(x_raw, k, 0)
            first = jnp.where(row8 < k, pltpu.roll(prev8, k, 0), xs_k[:SUBLANES, :])
            xs_k = jnp.concatenate([first, xs_k[SUBLANES:, :]], axis=0)
            pre = pre + xs_k * cw[CONV_WIDTH - 1 - k:CONV_WIDTH - k, :]
        tail_ref[...] = x_raw[q - SUBLANES:, :]
        act = _silu(pre)
        xs = act[:, :d_inner]
        bm = act[:, d_inner:d_inner + n_groups * d_state]
        cm = act[:, d_inner + n_groups * d_state:]

        dt = _softplus(dt_ref[rows, :] + dtb_ref[...])
        la = dt * neg_a
        a_cs = jnp.dot(tri, la, preferred_element_type=F32, precision=_HI)
        a_cs_t = _dot_tn(la, tri_t, precision=_HI)
        total = a_cs[q - 1:q, :]
        decay_s = jnp.exp(total - a_cs)
        ea = jnp.exp(a_cs)
        chunk_decay = jnp.exp(jnp.broadcast_to(total, (SUBLANES, LANES)))
        cd_feat = jnp.dot(chunk_decay, exp_mat, preferred_element_type=F32, precision=_HI)[:1, :]

        y_tiles = []
        for g in range(n_groups):
            b_g = bm[:, g * d_state:(g + 1) * d_state]
            c_g = cm[:, g * d_state:(g + 1) * d_state]
            b16 = b_g.astype(BF16)
            cb = _dot_nt(c_g.astype(BF16), b16)
            ht_g = ht_ref[g]
            w_tiles = []
            for jl in range(pairs_per_group):
                j = g * pairs_per_group + jl
                xs_pair = xs[:, j * LANES:(j + 1) * LANES]
                e0, e1 = 2 * j, 2 * j + 1
                dt_pair = jnp.where(head_mask[0], dt[:, e0:e0 + 1], dt[:, e1:e1 + 1])
                ds_pair = jnp.where(head_mask[0], decay_s[:, e0:e0 + 1], decay_s[:, e1:e1 + 1])
                xdt = xs_pair * dt_pair
                w_tiles.append((xdt * ds_pair).astype(BF16))
                ht_pair = ht_g[:, jl * LANES:(jl + 1) * LANES]
                y_pair = None
                for h in range(2):
                    e = 2 * j + h
                    seg = a_cs[:, e:e + 1] - a_cs_t[e:e + 1, :]
                    lmat = jnp.exp(jnp.where(causal, seg, NEG_INF))
                    lhs = jnp.concatenate([(cb * lmat).astype(BF16),
                                           (c_g * ea[:, e:e + 1]).astype(BF16)], axis=1)
                    rhs = jnp.concatenate([jnp.where(head_mask[h], xdt, 0.0).astype(BF16),
                                           jnp.where(head_mask[h], ht_pair, 0.0).astype(BF16)], axis=0)
                    yh = _dot(lhs, rhs)
                    y_pair = yh if y_pair is None else y_pair + yh
                y_tiles.append(y_pair + dskip_ref[:, j * LANES:(j + 1) * LANES] * xs_pair)
            w_g = jnp.concatenate(w_tiles, axis=1)
            ht_ref[g] = ht_g * cd_feat[:, g * gw:(g + 1) * gw] + _dot_tn(b16, w_g)

        y = jnp.concatenate(y_tiles, axis=1)
        yg = y * _silu(z_ref[rows, :])
        parts = []
        for g in range(n_groups):
            part = yg[:, g * gw:(g + 1) * gw]
            parts.append(part * lax.rsqrt(jnp.mean(part * part, axis=-1, keepdims=True) + EPS))
        y_ref[rows, :] = (jnp.concatenate(parts, axis=1) * gssm_ref[...]).astype(y_ref.dtype)
        return carry

    lax.fori_loop(0, ts // q, chunk, 0)

    @pl.when(step == pl.num_programs(1) - 1)
    def _():
        for g in range(n_groups):
            hout_ref[g * gw:(g + 1) * gw, :] = ht_ref[g].T


def _ssd_prompt(z, xbc, dtp, cw, cb, dtb, alog, dskip, gssm, *, n_heads, hd, n_groups, d_state, ts):
    b, s, d_inner = z.shape
    d_xbc = xbc.shape[-1]
    gw = d_inner // n_groups
    tok = lambda w: pl.BlockSpec((None, ts, w), lambda i, j: (i, j, 0))
    return pl.pallas_call(
        functools.partial(_ssd_prompt_body, n_heads=n_heads, hd=hd, n_groups=n_groups, d_state=d_state),
        grid=(b, s // ts),
        in_specs=[tok(d_inner), tok(d_xbc), tok(LANES),
                  _const_spec(cw.shape), _const_spec(cb.shape), _const_spec(dtb.shape),
                  _const_spec(alog.shape), _const_spec(dskip.shape), _const_spec(gssm.shape)],
        out_specs=[tok(d_inner), pl.BlockSpec((None, d_inner, d_state), lambda i, j: (i, 0, 0))],
        out_shape=[jax.ShapeDtypeStruct((b, s, d_inner), BF16),
                   jax.ShapeDtypeStruct((b, d_inner, d_state), F32)],
        scratch_shapes=[pltpu.VMEM((SUBLANES, d_xbc), F32),
                        pltpu.VMEM((n_groups, d_state, gw), F32)],
        compiler_params=_cparams("parallel", "arbitrary"),
        name="ssd_prompt",
    )(z, xbc, dtp, cw, cb, dtb, alog, dskip, gssm)


def _memkv_body(m_ref, g_ref, wk_ref, wv_ref, k_ref, v_ref):
    mn = _rms(m_ref[...], g_ref[...]).astype(BF16)
    k_ref[...] = _dot(mn, wk_ref[...])
    v_ref[...] = _dot(mn, wv_ref[...])


def _memkv(mem, g, wk, wv, *, tm):
    n, d = mem.shape
    dk = wk.shape[1]
    row = lambda w: pl.BlockSpec((tm, w), lambda i: (i, 0))
    return pl.pallas_call(
        _memkv_body,
        grid=(n // tm,),
        in_specs=[row(d), _const_spec((1, d)), _const_spec(wk.shape), _const_spec(wv.shape)],
        out_specs=[row(dk), row(dk)],
        out_shape=[jax.ShapeDtypeStruct((n, dk), F32)] * 2,
        compiler_params=_cparams("parallel"),
        name="memkv",
    )(mem, g, wk, wv)


def _post_prompt_body(x_ref, oa_ref, ys_ref, mk_ref, mv_ref, woa_ref, wos_ref, gc_ref, wcq_ref, wco_ref,
                      o_ref, *, n_heads_x):
    x2 = x_ref[...] + _dot(oa_ref[...], woa_ref[...]) + _dot(ys_ref[...], wos_ref[...])
    dx = x2.shape[1] // n_heads_x
    hq = _rms(x2, gc_ref[...]).astype(BF16)
    qx = (_dot(hq, wcq_ref[...]) * (dx ** -0.5)).astype(BF16)
    mk = mk_ref[...].astype(BF16)
    mv = mv_ref[...].astype(BF16)
    outs = []
    for h in range(n_heads_x):
        sl = slice(h * dx, (h + 1) * dx)
        s = _dot_nt(qx[:, sl], mk[:, sl])
        p = jnp.exp(s - jnp.max(s, axis=-1, keepdims=True))
        l = jnp.sum(p, axis=-1, keepdims=True)
        outs.append(_dot(p.astype(BF16), mv[:, sl]) / l)
    oc = jnp.concatenate(outs, axis=1).astype(BF16)
    o_ref[...] = x2 + _dot(oc, wco_ref[...])


def _post_prompt(x1, oa, ys, mk, mv, woa, wos, gc, wcq, wco, *, n_heads_x, tm):
    b, s, d = x1.shape
    n_mem = mk.shape[1]
    tok = lambda w: pl.BlockSpec((None, tm, w), lambda i, j: (i, j, 0))
    mem = pl.BlockSpec((None, n_mem, d), lambda i, j: (i, 0, 0))
    return pl.pallas_call(
        functools.partial(_post_prompt_body, n_heads_x=n_heads_x),
        grid=(b, s // tm),
        in_specs=[tok(d), tok(oa.shape[-1]), tok(ys.shape[-1]), mem, mem,
                  _const_spec(woa.shape), _const_spec(wos.shape), _const_spec(gc.shape),
                  _const_spec(wcq.shape), _const_spec(wco.shape)],
        out_specs=tok(d),
        out_shape=jax.ShapeDtypeStruct((b, s, d), F32),
        compiler_params=_cparams("parallel", "arbitrary"),
        name="post_prompt",
    )(x1, oa, ys, mk, mv, woa, wos, gc, wcq, wco)


def _attn_sample_body(q_ref, kn_ref, vn_ref, k1_ref, k4_ref, k16_ref, v1_ref, v4_ref, v16_ref, bias_ref,
                      o_ref, *, hd):
    bb = q_ref.shape[0]
    da = q_ref.shape[-1]
    n_heads = da // hd
    blk = N_BACK
    row = lax.broadcasted_iota(jnp.int32, (n_heads, da), 0)
    col = lax.broadcasted_iota(jnp.int32, (n_heads, da), 1)
    hmask = (col >= row * hd) & (col < (row + 1) * hd)
    k_refs = (k1_ref, k4_ref, k16_ref)
    v_refs = (v1_ref, v4_ref, v16_ref)
    n_pat = len(k_refs)

    def one(b, carry):
        qbd = jnp.where(hmask, q_ref[b], 0.0)
        qbd16 = qbd.astype(BF16)
        kn = kn_ref[b].astype(BF16).astype(F32)
        s_new = jnp.sum(qbd16.astype(F32) * kn, axis=-1, keepdims=True)
        scores = [_dot_nt(qbd16, k_refs[p][b].astype(BF16)) + bias_ref[p, :, :blk] for p in range(n_pat)]
        s_news = [s_new + bias_ref[p, :, blk:blk + 1] for p in range(n_pat)]
        m = s_news[0]
        for p in range(n_pat):
            m = jnp.maximum(m, jnp.maximum(s_news[p], jnp.max(scores[p], axis=-1, keepdims=True)))
        acc = jnp.zeros((n_heads, da), F32)
        den = jnp.zeros((n_heads, 1), F32)
        vn = vn_ref[b]
        for p in range(n_pat):
            e = jnp.exp(scores[p] - m)
            e_new = jnp.exp(s_news[p] - m)
            den = den + jnp.sum(e, axis=-1, keepdims=True) + e_new
            acc = acc + _dot(e.astype(BF16), v_refs[p][b].astype(BF16)) + e_new * vn
        out = jnp.where(hmask, acc / den, 0.0)
        o_ref[b] = jnp.sum(out, axis=0, keepdims=True)
        return carry

    lax.fori_loop(0, bb, one, 0)


def _attn_sample(q, kn, vn, kc, vc, bias_s, *, hd, bb):
    bs, _, da = q.shape
    w = kc.shape[1]
    blk = N_BACK
    n_pat = len(DILATIONS)
    assert w % (DILATIONS[-1] * blk) == 0, "window cache must hold every strided key"
    views = []
    specs = []
    for c in (kc, vc):
        for d in DILATIONS:
            views.append(c.reshape(bs, w // d, d * da))
            last = w // d // blk - 1
            specs.append(pl.BlockSpec((bb, blk, da), lambda i, last=last: (i, last, 0)))
    row = pl.BlockSpec((bb, 1, da), lambda i: (i, 0, 0))
    return pl.pallas_call(
        functools.partial(_attn_sample_body, hd=hd),
        grid=(bs // bb,),
        in_specs=[row, row, row] + specs + [_const_spec(bias_s.shape)],
        out_specs=row,
        out_shape=jax.ShapeDtypeStruct((bs, 1, da), F32),
        compiler_params=_cparams("parallel"),
        name="attn_sample",
    )(q, kn, vn, *views, bias_s)


def _ssd_sample_body(z_ref, xbc_ref, dt_ref, cc_ref, st_ref, cw_ref, cb_ref, dtb_ref, alog_ref, dskip_ref,
                     gssm_ref, y_ref, nconv_ref, nst_ref, *, n_groups):
    bb = z_ref.shape[0]
    n_pairs = z_ref.shape[1]
    pairs_per_group = n_pairs // n_groups
    gw = pairs_per_group * LANES
    row8 = lax.broadcasted_iota(jnp.int32, (n_pairs, 1), 0)
    neg_a = -jnp.exp(alog_ref[...])
    pad = jnp.zeros((LANES - 2 * n_pairs, LANES), F32)

    def one(b, carry):
        cc = cc_ref[b]
        xr = xbc_ref[b]
        pre = xr * cw_ref[CONV_WIDTH - 1] + cb_ref[...]
        for k in range(CONV_WIDTH - 1):
            pre = pre + cc[k] * cw_ref[k]
            nconv_ref[b, k] = cc[k + 1] if k + 1 < CONV_WIDTH - 1 else xr
        act = _silu(pre)
        xs = act[:n_pairs]
        bm = act[n_pairs:n_pairs + n_groups]
        cm = act[n_pairs + n_groups:]
        dt = _softplus(dt_ref[b] + dtb_ref[...])
        d_a = jnp.exp(dt * neg_a)
        cols = jnp.concatenate([xs * dt, d_a, pad], axis=0).T
        y = jnp.zeros((n_pairs, LANES), F32)
        for j in range(n_pairs):
            g = j // pairs_per_group
            rows = pl.ds(j * LANES, LANES)
            h_new = cols[:, n_pairs + j:n_pairs + j + 1] * st_ref[b, rows, :] \
                + cols[:, j:j + 1] * bm[g:g + 1, :]
            nst_ref[b, rows, :] = h_new
            c_sel = jnp.where(row8 == j, cm[g:g + 1, :], 0.0).astype(BF16)
            y = y + _dot_nt(c_sel, h_new.astype(BF16))
        y = y + dskip_ref[...] * xs
        yg = y * _silu(z_ref[b])
        ss = jnp.sum(yg * yg, axis=-1, keepdims=True)
        mean = jnp.zeros_like(ss)
        for g in range(n_groups):
            in_g = (row8 >= g * pairs_per_group) & (row8 < (g + 1) * pairs_per_group)
            tot = jnp.sum(jnp.where(in_g, ss, 0.0), axis=0, keepdims=True)
            mean = jnp.where(in_g, tot / gw, mean)
        y_ref[b] = yg * lax.rsqrt(mean + EPS) * gssm_ref[...]
        return carry

    lax.fori_loop(0, bb, one, 0)


def _ssd_sample(z, xbc, dt, cc, st, cw, cb, dtb, alog, dskip, gssm, *, n_groups, bb):
    bs = z.shape[0]
    lead = lambda a: pl.BlockSpec((bb,) + a.shape[1:], lambda i: (i,) + (0,) * (a.ndim - 1))
    return pl.pallas_call(
        functools.partial(_ssd_sample_body, n_groups=n_groups),
        grid=(bs // bb,),
        in_specs=[lead(z), lead(xbc), lead(dt), lead(cc), lead(st)]
        + [_const_spec(a.shape) for a in (cw, cb, dtb, alog, dskip, gssm)],
        out_specs=[lead(z), lead(cc), lead(st)],
        out_shape=[jax.ShapeDtypeStruct(z.shape, F32), jax.ShapeDtypeStruct(cc.shape, F32),
                   jax.ShapeDtypeStruct(st.shape, F32)],
        compiler_params=_cparams("parallel"),
        name="ssd_sample",
    )(z, xbc, dt, cc, st, cw, cb, dtb, alog, dskip, gssm)


def _mix_out_sample_body(x_ref, oa_ref, ys_ref, woa_ref, wos_ref, gc_ref, wcq_ref, x2_ref, qx_ref, *, scale):
    x2 = x_ref[...] + _dot(oa_ref[...].astype(BF16), woa_ref[...]) + _dot(ys_ref[...].astype(BF16), wos_ref[...])
    x2_ref[...] = x2
    qx_ref[...] = _dot(_rms(x2, gc_ref[...]).astype(BF16), wcq_ref[...]) * scale


def _mix_out_sample(x1, oa, ys, woa, wos, gc, wcq, *, scale):
    vm = pl.BlockSpec(memory_space=pltpu.VMEM)
    return pl.pallas_call(
        functools.partial(_mix_out_sample_body, scale=scale),
        in_specs=[vm] * 7,
        out_specs=[vm, vm],
        out_shape=[jax.ShapeDtypeStruct(x1.shape, F32), jax.ShapeDtypeStruct((x1.shape[0], wcq.shape[1]), F32)],
        compiler_params=pltpu.CompilerParams(vmem_limit_bytes=VMEM_LIMIT_BYTES),
        name="mix_out_sample",
    )(x1, oa, ys, woa, wos, gc, wcq)


def _cross_sample_body(q_ref, mk_ref, mv_ref, o_ref, *, n_heads_x):
    bb = q_ref.shape[0]
    d = q_ref.shape[-1]
    dx = d // n_heads_x
    row = lax.broadcasted_iota(jnp.int32, (SUBLANES, d), 0)
    col = lax.broadcasted_iota(jnp.int32, (SUBLANES, d), 1)
    hmask = (col >= row * dx) & (col < (row + 1) * dx)

    def one(b, carry):
        qbd = jnp.where(hmask, q_ref[b], 0.0).astype(BF16)
        s = _dot_nt(qbd, mk_ref[b].astype(BF16))
        p = jnp.exp(s - jnp.max(s, axis=-1, keepdims=True))
        l = jnp.sum(p, axis=-1, keepdims=True)
        o = _dot(p.astype(BF16), mv_ref[b].astype(BF16)) / l
        o_ref[b] = jnp.sum(jnp.where(hmask, o, 0.0), axis=0, keepdims=True)
        return carry

    lax.fori_loop(0, bb, one, 0)


def _cross_sample(qx, mk, mv, *, n_heads_x, bb):
    bs, n_mem, d = mk.shape
    row = pl.BlockSpec((bb, 1, d), lambda i: (i, 0, 0))
    mem = pl.BlockSpec((bb, n_mem, d), lambda i: (i, 0, 0))
    return pl.pallas_call(
        functools.partial(_cross_sample_body, n_heads_x=n_heads_x),
        grid=(bs // bb,),
        in_specs=[row, mem, mem],
        out_specs=row,
        out_shape=jax.ShapeDtypeStruct((bs, 1, d), F32),
        compiler_params=_cparams("parallel"),
        name="cross_sample",
    )(qx, mk, mv)


def _cross_out_sample_body(x_ref, oc_ref, wco_ref, o_ref):
    o_ref[...] = x_ref[...] + _dot(oc_ref[...].astype(BF16), wco_ref[...])


def _cross_out_sample(x2, oc, wco):
    vm = pl.BlockSpec(memory_space=pltpu.VMEM)
    return pl.pallas_call(
        _cross_out_sample_body,
        in_specs=[vm] * 3,
        out_specs=vm,
        out_shape=jax.ShapeDtypeStruct(x2.shape, F32),
        name="cross_out_sample",
    )(x2, oc, wco)


def _tile(n, target):
    t = min(n, target)
    while n % t or (t % SUBLANES and t != n):
        t -= 1
    return t


def kernel(x_prompt, x_sample, cache_win_k, cache_win_v, cache_conv, state_ssm, cache_mem_k, cache_mem_v, mem_prompt, rel_bias, g_ffn1, w1_gate, w1_up, w1_down, g_mix, w_in, conv_w, conv_b, dt_bias, a_log, d_skip, g_ssm, w_out, g_mem, w_ck, w_cv, g_cross, w_cq, w_co, g_ffn2, w2_gate, w2_up, w2_down, g_final):
    depth = g_ffn1.shape[0]
    b, s, d = x_prompt.shape
    bs, dec_seq, _ = x_sample.shape
    assert dec_seq == 1, "sample path handles one new token per sequence"
    n_heads_a, hd_a = cache_win_k.shape[-2:]
    d_attn = n_heads_a * hd_a
    n_heads_b, hd_b, d_state = state_ssm.shape[-3:]
    d_inner = n_heads_b * hd_b
    d_xbc = conv_w.shape[-1]
    n_groups = (d_xbc - d_inner) // (2 * d_state)
    n_mem, n_heads_x, hd_x = cache_mem_k.shape[-3:]
    w_buf = cache_win_k.shape[2]
    assert hd_a * 2 == LANES and hd_b * 2 == LANES and d_state == LANES
    assert s % (DILATIONS[-1] * N_BACK) == 0 and s >= CONV_WIDTH - 1

    cuts = (d_attn, 2 * d_attn, 3 * d_attn, 3 * d_attn + d_inner, 3 * d_attn + d_inner + d_xbc)
    d_in = w_in.shape[-1]
    d_in_pad = cuts[-1] + LANES
    n_pairs = d_inner // LANES
    xbc_rows = d_xbc // LANES

    def row(a):
        return a.reshape(1, -1)

    def per_feature(a):
        return jnp.repeat(a, hd_b).reshape(1, d_inner)

    def head_lanes(a):
        return jnp.pad(a, (0, LANES - n_heads_b)).reshape(1, LANES)

    bias_p, bias_s = _bias_tables(rel_bias)

    yp = x_prompt.reshape(b * s, d)
    ysm = x_sample.reshape(bs, d)
    tm_p = _tile(b * s, 512)
    tm_s = _tile(bs, 512)
    gfin = row(g_final)
    outs = [[] for _ in range(10)]
    for i in range(depth):
        bf = lambda a: a[i].astype(BF16)
        w1g, w1u, w1d = bf(w1_gate), bf(w1_up), bf(w1_down)
        w2g, w2u, w2d = bf(w2_gate), bf(w2_up), bf(w2_down)
        w_in_p = jnp.pad(w_in[i], ((0, 0), (0, d_in_pad - d_in))).astype(BF16)
        woa, wos = w_out[i, :d_attn].astype(BF16), w_out[i, d_attn:].astype(BF16)
        wck, wcv, wcq, wco = bf(w_ck), bf(w_cv), bf(w_cq), bf(w_co)
        last = i == depth - 1

        x1 = _ffn(yp, row(g_ffn1[i]), w1g, w1u, w1d, gfin, final_norm=False, tm=tm_p)
        q, k, v, z, xbc, dtp = _inproj(x1, row(g_mix[i]), w_in_p, cuts=cuts, q_scale=hd_a ** -0.5, tm=tm_p)
        to_seq = lambda a: a.reshape(b, s, a.shape[-1])
        oa = _attn_prompt(to_seq(q), to_seq(k), to_seq(v), bias_p, hd=hd_a)
        y_ssm, h_last = _ssd_prompt(
            to_seq(z), to_seq(xbc), to_seq(dtp), conv_w[i], row(conv_b[i]), head_lanes(dt_bias[i]),
            head_lanes(a_log[i]), per_feature(d_skip[i]), row(g_ssm[i]),
            n_heads=n_heads_b, hd=hd_b, n_groups=n_groups, d_state=d_state, ts=_tile(s, 512))
        mk, mv = _memkv(mem_prompt.reshape(b * n_mem, d), row(g_mem[i]), wck, wcv, tm=_tile(b * n_mem, 512))
        x3 = _post_prompt(to_seq(x1), oa, y_ssm, mk.reshape(b, n_mem, d), mv.reshape(b, n_mem, d),
                          woa, wos, row(g_cross[i]), wcq, wco, n_heads_x=n_heads_x, tm=_tile(s, 512))
        yp = _ffn(x3.reshape(b * s, d), row(g_ffn2[i]), w2g, w2u, w2d, gfin, final_norm=last, tm=tm_p)
        keep = min(MAX_DISTANCE, s)
        outs[0].append(to_seq(k)[:, s - keep:].reshape(b, keep, n_heads_a, hd_a))
        outs[1].append(to_seq(v)[:, s - keep:].reshape(b, keep, n_heads_a, hd_a))
        outs[2].append(to_seq(xbc)[:, s - (CONV_WIDTH - 1):])
        outs[3].append(h_last.reshape(b, n_heads_b, hd_b, d_state))
        outs[4].append(mk.reshape(b, n_mem, n_heads_x, hd_x))
        outs[5].append(mv.reshape(b, n_mem, n_heads_x, hd_x))

        x1s = _ffn(ysm, row(g_ffn1[i]), w1g, w1u, w1d, gfin, final_norm=False, tm=tm_s)
        qs, ks, vs, zs, xbcs, dts = _inproj(x1s, row(g_mix[i]), w_in_p, cuts=cuts, q_scale=hd_a ** -0.5, tm=tm_s)
        r3 = lambda a: a.reshape(bs, 1, a.shape[-1])
        oas = _attn_sample(r3(qs), r3(ks), r3(vs), cache_win_k[i].reshape(bs, w_buf, d_attn),
                           cache_win_v[i].reshape(bs, w_buf, d_attn), bias_s, hd=hd_a, bb=_tile(bs, 8))
        tiles = lambda a: a.reshape(a.shape[:-1] + (a.shape[-1] // LANES, LANES))
        dt_feat = jnp.repeat(dts[:, :n_heads_b], hd_b, axis=-1)
        ys_t, nconv, nstate = _ssd_sample(
            tiles(zs), tiles(xbcs), tiles(dt_feat), tiles(cache_conv[i]),
            state_ssm[i].reshape(bs, d_inner, d_state),
            tiles(conv_w[i]), tiles(conv_b[i]), tiles(jnp.repeat(dt_bias[i], hd_b)),
            tiles(jnp.repeat(a_log[i], hd_b)), tiles(jnp.repeat(d_skip[i], hd_b)), tiles(g_ssm[i]),
            n_groups=n_groups, bb=_tile(bs, 8))
        x2s, qxs = _mix_out_sample(x1s, oas.reshape(bs, d_attn), ys_t.reshape(bs, d_inner), woa, wos,
                                   row(g_cross[i]), wcq, scale=hd_x ** -0.5)
        ocs = _cross_sample(r3(qxs), cache_mem_k[i].reshape(bs, n_mem, d), cache_mem_v[i].reshape(bs, n_mem, d),
                            n_heads_x=n_heads_x, bb=_tile(bs, 8))
        x3s = _cross_out_sample(x2s, ocs.reshape(bs, d), wco)
        ysm = _ffn(x3s, row(g_ffn2[i]), w2g, w2u, w2d, gfin, final_norm=last, tm=tm_s)
        outs[6].append(ks.reshape(bs, 1, n_heads_a, hd_a))
        outs[7].append(vs.reshape(bs, 1, n_heads_a, hd_a))
        outs[8].append(nconv.reshape(bs, CONV_WIDTH - 1, d_xbc))
        outs[9].append(nstate.reshape(bs, n_heads_b, hd_b, d_state))

    return (yp.reshape(b, s, d), ysm.reshape(bs, 1, d)) + tuple(jnp.stack(o) for o in outs)
```

```python
import functools
import math

import jax
import jax.numpy as jnp
from jax import lax
from jax.experimental import pallas as pl
from jax.experimental.pallas import tpu as pltpu

F32 = jnp.float32
BF16 = jnp.bfloat16

EPS = 1e-6
NEG_INF = -1e30
DILATIONS = (1, 4, 16)
N_BACK = 128
MAX_DISTANCE = 2048
SSD_CHUNK = 128
CONV_WIDTH = 4

LANES = 128
SUBLANES = 8
VMEM_LIMIT_BYTES = 56 * 1024 * 1024

_HI = lax.Precision.HIGHEST


def _cparams(*sem):
    return pltpu.CompilerParams(dimension_semantics=sem, vmem_limit_bytes=VMEM_LIMIT_BYTES)


def _const_spec(shape):
    nd = len(shape)
    return pl.BlockSpec(shape, lambda *_: (0,) * nd, pipeline_mode=pl.Buffered(1))


def _rms(x, g):
    return x * lax.rsqrt(jnp.mean(x * x, axis=-1, keepdims=True) + EPS) * g


def _silu(x):
    return x * (1.0 / (1.0 + jnp.exp(-x)))


def _softplus(x):
    return jnp.maximum(x, 0.0) + jnp.log1p(jnp.exp(-jnp.abs(x)))


def _dot(a, b):
    return jnp.dot(a, b, preferred_element_type=F32)


def _dot_nt(a, b):
    return lax.dot_general(a, b, (((1,), (1,)), ((), ())), preferred_element_type=F32)


def _dot_tn(a, b, precision=None):
    return lax.dot_general(a, b, (((0,), (0,)), ((), ())), preferred_element_type=F32,
                           precision=precision)


def _ffn_body(x_ref, g_ref, wg_ref, wu_ref, wd_ref, gf_ref, o_ref, *, final_norm):
    x = x_ref[...]
    xn = _rms(x, g_ref[...]).astype(BF16)
    gate = _dot(xn, wg_ref[...])
    up = _dot(xn, wu_ref[...])
    h = (_silu(gate) * up).astype(BF16)
    y = x + 0.5 * _dot(h, wd_ref[...])
    if final_norm:
        y = _rms(y, gf_ref[...])
    o_ref[...] = y


def _ffn(x, g, wg, wu, wd, g_final, *, final_norm, tm):
    n, d = x.shape
    ff = wg.shape[1]
    row = pl.BlockSpec((tm, d), lambda i: (i, 0))
    return pl.pallas_call(
        functools.partial(_ffn_body, final_norm=final_norm),
        grid=(n // tm,),
        in_specs=[row, _const_spec((1, d)), _const_spec((d, ff)), _const_spec((d, ff)),
                  _const_spec((ff, d)), _const_spec((1, d))],
        out_specs=row,
        out_shape=jax.ShapeDtypeStruct((n, d), F32),
        compiler_params=_cparams("parallel"),
        name="ffn",
    )(x, g, wg, wu, wd, g_final)


def _inproj_body(x_ref, g_ref, w_ref, q_ref, k_ref, v_ref, z_ref, xbc_ref, dt_ref, *t_refs, cuts, q_scale):
    xn = _rms(x_ref[...], g_ref[...]).astype(BF16)
    proj = _dot(xn, w_ref[...])
    c0, c1, c2, c3, c4 = cuts
    q_ref[...] = proj[:, :c0] * q_scale
    k_ref[...] = proj[:, c0:c1]
    v_ref[...] = proj[:, c1:c2]
    z_ref[...] = proj[:, c2:c3]
    xbc_ref[...] = proj[:, c3:c4]
    dt_ref[...] = proj[:, c4:]
    if t_refs:
        kt_ref, vt_ref = t_refs
        kt_ref[...] = proj[:, c0:c1].T
        vt_ref[...] = proj[:, c1:c2].T


def _inproj(x, g, w_pad, *, cuts, q_scale, tm, seq=None):
    n, d = x.shape
    wp = w_pad.shape[1]
    c0, c1, c2, c3, c4 = cuts
    widths = (c0, c1 - c0, c2 - c1, c3 - c2, c4 - c3, wp - c4)
    row = lambda w: pl.BlockSpec((tm, w), lambda i: (i, 0))
    out_specs = [row(w) for w in widths]
    out_shape = [jax.ShapeDtypeStruct((n, w), F32) for w in widths]
    if seq is not None:
        per_seq = seq // tm
        for w in widths[1:3]:
            out_specs.append(pl.BlockSpec((None, w, tm), lambda i: (i // per_seq, 0, i % per_seq)))
            out_shape.append(jax.ShapeDtypeStruct((n // seq, w, seq), F32))
    return pl.pallas_call(
        functools.partial(_inproj_body, cuts=cuts, q_scale=q_scale),
        grid=(n // tm,),
        in_specs=[row(d), _const_spec((1, d)), _const_spec((d, wp))],
        out_specs=out_specs,
        out_shape=out_shape,
        compiler_params=_cparams("parallel"),
        name="inproj",
    )(x, g, w_pad)


def _t5_bucket(dist, num_buckets):
    max_exact = num_buckets // 2
    d = jnp.maximum(dist, 0)
    df = jnp.maximum(d, 1).astype(F32)
    large = max_exact + (jnp.log(df / max_exact) / math.log(MAX_DISTANCE / max_exact)
                         * (num_buckets - max_exact)).astype(jnp.int32)
    large = jnp.minimum(large, num_buckets - 1)
    return jnp.where(d < max_exact, d, large)


def _bias_body(rb_ref, idxp_ref, idxs_ref, cnt_ref, bp_ref, bs_ref, *, n_buckets, n_heads):
    n_pat = idxp_ref.shape[0]
    blk = idxp_ref.shape[1]
    qi = lax.broadcasted_iota(jnp.int32, (blk, 2 * blk), 0)
    kj = lax.broadcasted_iota(jnp.int32, (blk, 2 * blk), 1)
    delta = qi + blk - kj
    band = (delta >= 0) & (delta <= N_BACK)

    def lookup(idx, h):
        acc = jnp.zeros(idx.shape, F32)
        for c in range(n_buckets):
            acc = jnp.where(idx == c, rb_ref[c, h], acc)
        return acc

    for h in range(n_heads):
        for p in range(n_pat):
            bp_ref[p, h] = jnp.where(band, lookup(idxp_ref[p], h), NEG_INF)
        bs_ref[h:h + 1, :] = jnp.where(cnt_ref[...] > 0.0, lookup(idxs_ref[...], h), NEG_INF)


def _bias_tables(rel_bias, w_buf):
    n_buckets, n_heads = rel_bias.shape
    blk = N_BACK
    n_pat = len(DILATIONS)
    qi = jnp.arange(blk)[:, None]
    kj = jnp.arange(2 * blk)[None, :]
    delta = qi + blk - kj
    idx_p = jnp.stack([_t5_bucket(delta * d, n_buckets) for d in DILATIONS]).astype(jnp.int32)
    dist = jnp.concatenate([w_buf - jnp.arange(w_buf), jnp.zeros((LANES,), jnp.int32)])
    idx_s = _t5_bucket(dist, n_buckets).astype(jnp.int32)[None, :]
    count = sum(((dist % d == 0) & (dist <= d * N_BACK)).astype(F32) for d in DILATIONS)[None, :]
    vm = pl.BlockSpec(memory_space=pltpu.VMEM)
    bias_p, bias_s = pl.pallas_call(
        functools.partial(_bias_body, n_buckets=n_buckets, n_heads=n_heads),
        in_specs=[pl.BlockSpec(memory_space=pltpu.SMEM), vm, vm, vm],
        out_specs=[vm, vm],
        out_shape=[jax.ShapeDtypeStruct((n_pat, n_heads, blk, 2 * blk), F32),
                   jax.ShapeDtypeStruct((n_heads, w_buf + LANES), F32)],
        name="bias_tables",
    )(rel_bias, idx_p, idx_s, count)
    return bias_p, bias_s, count


def _attn_prompt_body(q_ref, k_ref, v_ref, bias_ref, o_ref, acc2, l2, m2, acc3, l3, m3, *, hd):
    seq = q_ref.shape[0]
    blk = N_BACK
    lane = lax.broadcasted_iota(jnp.int32, (1, LANES), 1)
    head_mask = (lane < hd, lane >= hd)

    def block(qb, kb, vb, bias_of_head):
        kb16 = kb.astype(BF16)
        out = None
        ms = []
        for h in range(2):
            qh = jnp.where(head_mask[h], qb, 0.0).astype(BF16)
            s = _dot_nt(qh, kb16) + bias_of_head(h)
            m = jnp.max(s, axis=-1, keepdims=True)
            p = jnp.exp(s - m).astype(BF16)
            vh = jnp.where(head_mask[h], vb, 0.0).astype(BF16)
            ones_h = jnp.broadcast_to(jnp.where(head_mask[h], 1.0, 0.0).astype(BF16), vh.shape)
            oh = _dot(p, jnp.concatenate([vh, ones_h], axis=1))
            out = oh if out is None else out + oh
            ms.append(m)
        m_full = jnp.where(head_mask[0], ms[0], ms[1])
        return out[:, :LANES], out[:, LANES:], m_full

    def cur_bias(p):
        return lambda h: bias_ref[p, h, :, blk:]

    def full_bias(p):
        return lambda h: bias_ref[p, h]

    d3 = DILATIONS[2]

    def pat3(r, carry):
        rows = pl.ds(r, blk, stride=d3)
        a, l, m = block(q_ref[rows, :], k_ref[rows, :], v_ref[rows, :], cur_bias(2))
        acc3[rows, :] = a
        l3[rows, :] = l
        m3[rows, :] = m
        return carry

    lax.fori_loop(0, d3, pat3, 0)

    d2 = DILATIONS[1]
    nb2 = seq // (d2 * blk)

    def pat2(r, carry):
        for i in range(nb2):
            rows = pl.ds(r + d2 * blk * i, blk, stride=d2)
            if i == 0:
                krows, bias = rows, cur_bias(1)
            else:
                krows, bias = pl.ds(r + d2 * blk * (i - 1), 2 * blk, stride=d2), full_bias(1)
            a, l, m = block(q_ref[rows, :], k_ref[krows, :], v_ref[krows, :], bias)
            acc2[rows, :] = a
            l2[rows, :] = l
            m2[rows, :] = m
        return carry

    lax.fori_loop(0, d2, pat2, 0)

    def finish(rows, a1, l1, m1):
        a2_, l2_, m2_ = acc2[rows, :], l2[rows, :], m2[rows, :]
        a3_, l3_, m3_ = acc3[rows, :], l3[rows, :], m3[rows, :]
        mx = jnp.maximum(jnp.maximum(m1, m2_), m3_)
        w1, w2, w3 = jnp.exp(m1 - mx), jnp.exp(m2_ - mx), jnp.exp(m3_ - mx)
        num = w1 * a1 + w2 * a2_ + w3 * a3_
        den = w1 * l1 + w2 * l2_ + w3 * l3_
        o_ref[rows, :] = (num / den).astype(o_ref.dtype)

    rows0 = pl.ds(0, blk)
    finish(rows0, *block(q_ref[rows0, :], k_ref[rows0, :], v_ref[rows0, :], cur_bias(0)))

    def pat1(i, carry):
        start = pl.multiple_of(i * blk, blk)
        rows = pl.ds(start, blk)
        krows = pl.ds(pl.multiple_of(start - blk, blk), 2 * blk)
        finish(rows, *block(q_ref[rows, :], k_ref[krows, :], v_ref[krows, :], full_bias(0)))
        return carry

    lax.fori_loop(1, seq // blk, pat1, 0)


def _attn_prompt(q, k, v, bias_p, *, hd):
    b, s, da = q.shape
    n_pairs = da // LANES
    n_pat = bias_p.shape[0]
    blk = N_BACK
    tok = pl.BlockSpec((None, s, LANES), lambda i, j: (i, 0, j))
    return pl.pallas_call(
        functools.partial(_attn_prompt_body, hd=hd),
        grid=(b, n_pairs),
        in_specs=[tok, tok, tok,
                  pl.BlockSpec((n_pat, 2, blk, 2 * blk), lambda i, j: (0, j, 0, 0))],
        out_specs=tok,
        out_shape=jax.ShapeDtypeStruct((b, s, da), BF16),
        scratch_shapes=[pltpu.VMEM((s, LANES), F32) for _ in range(6)],
        compiler_params=_cparams("parallel", "arbitrary"),
        name="attn_prompt",
    )(q, k, v, bias_p)


def _ssd_prompt_body(z_ref, xbc_ref, dt_ref, cw_ref, cb_ref, dtb_ref, alog_ref, dskip_ref, gssm_ref,
                     y_ref, hout_ref, tail_ref, ht_ref, *, n_heads, hd, n_groups, d_state):
    step = pl.program_id(1)
    ts = z_ref.shape[0]
    q = SSD_CHUNK
    d_inner = n_heads * hd
    gw = d_inner // n_groups
    pairs_per_group = gw // LANES
    n_pairs = d_inner // LANES

    @pl.when(step == 0)
    def _():
        tail_ref[...] = jnp.zeros_like(tail_ref)
        ht_ref[...] = jnp.zeros_like(ht_ref)

    lane = lax.broadcasted_iota(jnp.int32, (1, LANES), 1)
    head_mask = (lane < hd, lane >= hd)
    row_q = lax.broadcasted_iota(jnp.int32, (q, q), 0)
    col_q = lax.broadcasted_iota(jnp.int32, (q, q), 1)
    causal = row_q >= col_q
    tri = jnp.where(causal, 1.0, 0.0).astype(F32)
    tri_t = jnp.where(row_q <= col_q, 1.0, 0.0).astype(F32)
    row8 = lax.broadcasted_iota(jnp.int32, (SUBLANES, 1), 0)
    e_row = lax.broadcasted_iota(jnp.int32, (LANES, d_inner), 0)
    e_col = lax.broadcasted_iota(jnp.int32, (LANES, d_inner), 1)
    exp_mat = jnp.where((e_col >= e_row * hd) & (e_col < (e_row + 1) * hd), 1.0, 0.0).astype(F32)

    cw = cw_ref[...]
    neg_a = -jnp.exp(alog_ref[...])

    def chunk(c, carry):
        t0 = pl.multiple_of(c * q, q)
        rows = pl.ds(t0, q)
        x_raw = xbc_ref[rows, :]
        prev8 = tail_ref[...]
        pre = x_raw * cw[CONV_WIDTH - 1:CONV_WIDTH, :] + cb_ref[...]
        for k in range(1, CONV_WIDTH):
            xs_k = pltpu.roll(x_raw, k, 0)
            first = jnp.where(row8 < k, pltpu.roll(prev8, k, 0), xs_k[:SUBLANES, :])
            xs_k = jnp.concatenate([first, xs_k[SUBLANES:, :]], axis=0)
            pre = pre + xs_k * cw[CONV_WIDTH - 1 - k:CONV_WIDTH - k, :]
        tail_ref[...] = x_raw[q - SUBLANES:, :]
        act = _silu(pre)
        xs = act[:, :d_inner]
        bm = act[:, d_inner:d_inner + n_groups * d_state]
        cm = act[:, d_inner + n_groups * d_state:]

        dt = _softplus(dt_ref[rows, :] + dtb_ref[...])
        la = dt * neg_a
        a_cs = jnp.dot(tri, la, preferred_element_type=F32, precision=_HI)
        a_cs_t = _dot_tn(la, tri_t, precision=_HI)
        total = a_cs[q - 1:q, :]
        decay_s = jnp.exp(total - a_cs)
        ea = jnp.exp(a_cs)
        chunk_decay = jnp.exp(jnp.broadcast_to(total, (SUBLANES, LANES)))
        cd_feat = jnp.dot(chunk_decay, exp_mat, preferred_element_type=F32, precision=_HI)[:1, :]

        y_tiles = []
        for g in range(n_groups):
            b_g = bm[:, g * d_state:(g + 1) * d_state]
            c_g = cm[:, g * d_state:(g + 1) * d_state]
            b16 = b_g.astype(BF16)
            cb = _dot_nt(c_g.astype(BF16), b16)
            ht_g = ht_ref[g]
            w_tiles = []
            for jl in range(pairs_per_group):
                j = g * pairs_per_group + jl
                xs_pair = xs[:, j * LANES:(j + 1) * LANES]
                e0, e1 = 2 * j, 2 * j + 1
                dt_pair = jnp.where(head_mask[0], dt[:, e0:e0 + 1], dt[:, e1:e1 + 1])
                ds_pair = jnp.where(head_mask[0], decay_s[:, e0:e0 + 1], decay_s[:, e1:e1 + 1])
                xdt = xs_pair * dt_pair
                w_tiles.append((xdt * ds_pair).astype(BF16))
                ht_pair = ht_g[:, jl * LANES:(jl + 1) * LANES]
                y_pair = None
                for h in range(2):
                    e = 2 * j + h
                    seg = a_cs[:, e:e + 1] - a_cs_t[e:e + 1, :]
                    lmat = jnp.exp(jnp.where(causal, seg, NEG_INF))
                    lhs = jnp.concatenate([(cb * lmat).astype(BF16),
                                           (c_g * ea[:, e:e + 1]).astype(BF16)], axis=1)
                    rhs = jnp.concatenate([jnp.where(head_mask[h], xdt, 0.0).astype(BF16),
                                           jnp.where(head_mask[h], ht_pair, 0.0).astype(BF16)], axis=0)
                    yh = _dot(lhs, rhs)
                    y_pair = yh if y_pair is None else y_pair + yh
                y_tiles.append(y_pair + dskip_ref[:, j * LANES:(j + 1) * LANES] * xs_pair)
            w_g = jnp.concatenate(w_tiles, axis=1)
            ht_ref[g] = ht_g * cd_feat[:, g * gw:(g + 1) * gw] + _dot_tn(b16, w_g)

        y = jnp.concatenate(y_tiles, axis=1)
        yg = y * _silu(z_ref[rows, :])
        parts = []
        for g in range(n_groups):
            part = yg[:, g * gw:(g + 1) * gw]
            parts.append(part * lax.rsqrt(jnp.mean(part * part, axis=-1, keepdims=True) + EPS))
        y_ref[rows, :] = (jnp.concatenate(parts, axis=1) * gssm_ref[...]).astype(y_ref.dtype)
        return carry

    lax.fori_loop(0, ts // q, chunk, 0)

    @pl.when(step == pl.num_programs(1) - 1)
    def _():
        for g in range(n_groups):
            hout_ref[g * gw:(g + 1) * gw, :] = ht_ref[g].T


def _ssd_prompt(z, xbc, dtp, cw, cb, dtb, alog, dskip, gssm, *, n_heads, hd, n_groups, d_state, ts):
    b, s, d_inner = z.shape
    d_xbc = xbc.shape[-1]
    gw = d_inner // n_groups
    tok = lambda w: pl.BlockSpec((None, ts, w), lambda i, j: (i, j, 0))
    return pl.pallas_call(
        functools.partial(_ssd_prompt_body, n_heads=n_heads, hd=hd, n_groups=n_groups, d_state=d_state),
        grid=(b, s // ts),
        in_specs=[tok(d_inner), tok(d_xbc), tok(LANES),
                  _const_spec(cw.shape), _const_spec(cb.shape), _const_spec(dtb.shape),
                  _const_spec(alog.shape), _const_spec(dskip.shape), _const_spec(gssm.shape)],
        out_specs=[tok(d_inner), pl.BlockSpec((None, d_inner, d_state), lambda i, j: (i, 0, 0))],
        out_shape=[jax.ShapeDtypeStruct((b, s, d_inner), BF16),
                   jax.ShapeDtypeStruct((b, d_inner, d_state), F32)],
        scratch_shapes=[pltpu.VMEM((SUBLANES, d_xbc), F32),
                        pltpu.VMEM((n_groups, d_state, gw), F32)],
        compiler_params=_cparams("parallel", "arbitrary"),
        name="ssd_prompt",
    )(z, xbc, dtp, cw, cb, dtb, alog, dskip, gssm)


def _memkv_body(m_ref, g_ref, wk_ref, wv_ref, k_ref, v_ref):
    mn = _rms(m_ref[...], g_ref[...]).astype(BF16)
    k_ref[...] = _dot(mn, wk_ref[...])
    v_ref[...] = _dot(mn, wv_ref[...])


def _memkv(mem, g, wk, wv, *, tm):
    n, d = mem.shape
    dk = wk.shape[1]
    row = lambda w: pl.BlockSpec((tm, w), lambda i: (i, 0))
    return pl.pallas_call(
        _memkv_body,
        grid=(n // tm,),
        in_specs=[row(d), _const_spec((1, d)), _const_spec(wk.shape), _const_spec(wv.shape)],
        out_specs=[row(dk), row(dk)],
        out_shape=[jax.ShapeDtypeStruct((n, dk), F32)] * 2,
        compiler_params=_cparams("parallel"),
        name="memkv",
    )(mem, g, wk, wv)


def _post_prompt_body(x_ref, oa_ref, ys_ref, mk_ref, mv_ref, woa_ref, wos_ref, gc_ref, wcq_ref, wco_ref,
                      o_ref, *, n_heads_x):
    x2 = x_ref[...] + _dot(oa_ref[...], woa_ref[...]) + _dot(ys_ref[...], wos_ref[...])
    dx = x2.shape[1] // n_heads_x
    hq = _rms(x2, gc_ref[...]).astype(BF16)
    qx = (_dot(hq, wcq_ref[...]) * (dx ** -0.5)).astype(BF16)
    mk = mk_ref[...].astype(BF16)
    mv = mv_ref[...].astype(BF16)
    outs = []
    for h in range(n_heads_x):
        sl = slice(h * dx, (h + 1) * dx)
        s = _dot_nt(qx[:, sl], mk[:, sl])
        p = jnp.exp(s - jnp.max(s, axis=-1, keepdims=True))
        l = jnp.sum(p, axis=-1, keepdims=True)
        outs.append(_dot(p.astype(BF16), mv[:, sl]) / l)
    oc = jnp.concatenate(outs, axis=1).astype(BF16)
    o_ref[...] = x2 + _dot(oc, wco_ref[...])


def _post_prompt(x1, oa, ys, mk, mv, woa, wos, gc, wcq, wco, *, n_heads_x, tm):
    b, s, d = x1.shape
    n_mem = mk.shape[1]
    tok = lambda w: pl.BlockSpec((None, tm, w), lambda i, j: (i, j, 0))
    mem = pl.BlockSpec((None, n_mem, d), lambda i, j: (i, 0, 0))
    return pl.pallas_call(
        functools.partial(_post_prompt_body, n_heads_x=n_heads_x),
        grid=(b, s // tm),
        in_specs=[tok(d), tok(oa.shape[-1]), tok(ys.shape[-1]), mem, mem,
                  _const_spec(woa.shape), _const_spec(wos.shape), _const_spec(gc.shape),
                  _const_spec(wcq.shape), _const_spec(wco.shape)],
        out_specs=tok(d),
        out_shape=jax.ShapeDtypeStruct((b, s, d), F32),
        compiler_params=_cparams("parallel", "arbitrary"),
        name="post_prompt",
    )(x1, oa, ys, mk, mv, woa, wos, gc, wcq, wco)


def _attn_sample_body(q_ref, kn_ref, vn_ref, kt_ref, vt_ref, bias_ref, cnt_ref, o_ref, *, hd):
    da, w = kt_ref.shape
    n_heads = da // hd
    row = lax.broadcasted_iota(jnp.int32, (n_heads, da), 0)
    col = lax.broadcasted_iota(jnp.int32, (n_heads, da), 1)
    hmask = (col >= row * hd) & (col < (row + 1) * hd)
    qbd16 = jnp.where(hmask, q_ref[...], 0.0).astype(BF16)
    s = _dot(qbd16, kt_ref[...].astype(BF16)) + bias_ref[:, :w]
    kn = kn_ref[...].astype(BF16).astype(F32)
    s_new = jnp.sum(qbd16.astype(F32) * kn, axis=-1, keepdims=True) + bias_ref[:, w:w + 1]
    m = jnp.maximum(jnp.max(s, axis=-1, keepdims=True), s_new)
    e = jnp.exp(s - m) * cnt_ref[:, :w]
    e_new = jnp.exp(s_new - m) * cnt_ref[:, w:w + 1]
    den = jnp.sum(e, axis=-1, keepdims=True) + e_new
    acc = _dot_nt(e.astype(BF16), vt_ref[...].astype(BF16)) + e_new * vn_ref[...]
    o_ref[...] = jnp.sum(jnp.where(hmask, acc / den, 0.0), axis=0, keepdims=True)


def _attn_sample(q, kn, vn, kct, vct, bias_s, count, *, hd):
    bs, _, da = q.shape
    w = kct.shape[-1]
    row = pl.BlockSpec((None, 1, da), lambda i: (i, 0, 0))
    cache = pl.BlockSpec((None, da, w), lambda i: (i, 0, 0))
    return pl.pallas_call(
        functools.partial(_attn_sample_body, hd=hd),
        grid=(bs,),
        in_specs=[row, row, row, cache, cache, _const_spec(bias_s.shape), _const_spec(count.shape)],
        out_specs=row,
        out_shape=jax.ShapeDtypeStruct((bs, 1, da), F32),
        compiler_params=_cparams("parallel"),
        name="attn_sample",
    )(q, kn, vn, kct, vct, bias_s, count)


def _ssd_sample_body(z_ref, xbc_ref, dt_ref, cc_ref, st_ref, cw_ref, cb_ref, dtb_ref, alog_ref, dskip_ref,
                     gssm_ref, y_ref, nconv_ref, nst_ref, *, n_groups):
    bb = z_ref.shape[0]
    n_pairs = z_ref.shape[1]
    pairs_per_group = n_pairs // n_groups
    gw = pairs_per_group * LANES
    row8 = lax.broadcasted_iota(jnp.int32, (n_pairs, 1), 0)
    neg_a = -jnp.exp(alog_ref[...])
    pad = jnp.zeros((LANES - 2 * n_pairs, LANES), F32)

    def one(b, carry):
        cc = cc_ref[b]
        xr = xbc_ref[b]
        pre = xr * cw_ref[CONV_WIDTH - 1] + cb_ref[...]
        for k in range(CONV_WIDTH - 1):
            pre = pre + cc[k] * cw_ref[k]
            nconv_ref[b, k] = cc[k + 1] if k + 1 < CONV_WIDTH - 1 else xr
        act = _silu(pre)
        xs = act[:n_pairs]
        bm = act[n_pairs:n_pairs + n_groups]
        cm = act[n_pairs + n_groups:]
        dt = _softplus(dt_ref[b] + dtb_ref[...])
        d_a = jnp.exp(dt * neg_a)
        cols = jnp.concatenate([xs * dt, d_a, pad], axis=0).T
        y = jnp.zeros((n_pairs, LANES), F32)
        for j in range(n_pairs):
            g = j // pairs_per_group
            rows = pl.ds(j * LANES, LANES)
            h_new = cols[:, n_pairs + j:n_pairs + j + 1] * st_ref[b, rows, :] \
                + cols[:, j:j + 1] * bm[g:g + 1, :]
            nst_ref[b, rows, :] = h_new
            c_sel = jnp.where(row8 == j, cm[g:g + 1, :], 0.0).astype(BF16)
            y = y + _dot_nt(c_sel, h_new.astype(BF16))
        y = y + dskip_ref[...] * xs
        yg = y * _silu(z_ref[b])
        ss = jnp.sum(yg * yg, axis=-1, keepdims=True)
        mean = jnp.zeros_like(ss)
        for g in range(n_groups):
            in_g = (row8 >= g * pairs_per_group) & (row8 < (g + 1) * pairs_per_group)
            tot = jnp.sum(jnp.where(in_g, ss, 0.0), axis=0, keepdims=True)
            mean = jnp.where(in_g, tot / gw, mean)
        y_ref[b] = yg * lax.rsqrt(mean + EPS) * gssm_ref[...]
        return carry

    lax.fori_loop(0, bb, one, 0)


def _ssd_sample(z, xbc, dt, cc, st, cw, cb, dtb, alog, dskip, gssm, *, n_groups, bb):
    bs = z.shape[0]
    lead = lambda a: pl.BlockSpec((bb,) + a.shape[1:], lambda i: (i,) + (0,) * (a.ndim - 1))
    return pl.pallas_call(
        functools.partial(_ssd_sample_body, n_groups=n_groups),
        grid=(bs // bb,),
        in_specs=[lead(z), lead(xbc), lead(dt), lead(cc), lead(st)]
        + [_const_spec(a.shape) for a in (cw, cb, dtb, alog, dskip, gssm)],
        out_specs=[lead(z), lead(cc), lead(st)],
        out_shape=[jax.ShapeDtypeStruct(z.shape, F32), jax.ShapeDtypeStruct(cc.shape, F32),
                   jax.ShapeDtypeStruct(st.shape, F32)],
        compiler_params=_cparams("parallel"),
        name="ssd_sample",
    )(z, xbc, dt, cc, st, cw, cb, dtb, alog, dskip, gssm)


def _mix_out_sample_body(x_ref, oa_ref, ys_ref, woa_ref, wos_ref, gc_ref, wcq_ref, x2_ref, qx_ref, *, scale):
    x2 = x_ref[...] + _dot(oa_ref[...].astype(BF16), woa_ref[...]) + _dot(ys_ref[...].astype(BF16), wos_ref[...])
    x2_ref[...] = x2
    qx_ref[...] = _dot(_rms(x2, gc_ref[...]).astype(BF16), wcq_ref[...]) * scale


def _mix_out_sample(x1, oa, ys, woa, wos, gc, wcq, *, scale):
    vm = pl.BlockSpec(memory_space=pltpu.VMEM)
    return pl.pallas_call(
        functools.partial(_mix_out_sample_body, scale=scale),
        in_specs=[vm] * 7,
        out_specs=[vm, vm],
        out_shape=[jax.ShapeDtypeStruct(x1.shape, F32), jax.ShapeDtypeStruct((x1.shape[0], wcq.shape[1]), F32)],
        compiler_params=pltpu.CompilerParams(vmem_limit_bytes=VMEM_LIMIT_BYTES),
        name="mix_out_sample",
    )(x1, oa, ys, woa, wos, gc, wcq)


def _cross_sample_body(q_ref, mk_ref, mv_ref, o_ref):
    bb = q_ref.shape[0]

    def one(b, carry):
        s = jnp.sum(mk_ref[b] * q_ref[b][None], axis=-1, keepdims=True)
        p = jnp.exp(s - jnp.max(s, axis=0, keepdims=True))
        l = jnp.sum(p, axis=0)
        o_ref[b] = jnp.sum(p * mv_ref[b], axis=0) / l
        return carry

    lax.fori_loop(0, bb, one, 0)


def _cross_sample(qx, mk, mv, *, bb):
    bs, n_mem, nh, dx = mk.shape
    row = pl.BlockSpec((bb, nh, dx), lambda i: (i, 0, 0))
    mem = pl.BlockSpec((bb, n_mem, nh, dx), lambda i: (i, 0, 0, 0))
    return pl.pallas_call(
        _cross_sample_body,
        grid=(bs // bb,),
        in_specs=[row, mem, mem],
        out_specs=row,
        out_shape=jax.ShapeDtypeStruct((bs, nh, dx), F32),
        compiler_params=_cparams("parallel"),
        name="cross_sample",
    )(qx, mk, mv)


def _cross_out_sample_body(x_ref, oc_ref, wco_ref, o_ref):
    o_ref[...] = x_ref[...] + _dot(oc_ref[...].astype(BF16), wco_ref[...])


def _cross_out_sample(x2, oc, wco):
    vm = pl.BlockSpec(memory_space=pltpu.VMEM)
    return pl.pallas_call(
        _cross_out_sample_body,
        in_specs=[vm] * 3,
        out_specs=vm,
        out_shape=jax.ShapeDtypeStruct(x2.shape, F32),
        name="cross_out_sample",
    )(x2, oc, wco)


def _tile(n, target):
    t = min(n, target)
    while n % t or (t % SUBLANES and t != n):
        t -= 1
    return t


def kernel(x_prompt, x_sample, cache_win_k, cache_win_v, cache_conv, state_ssm, cache_mem_k, cache_mem_v, mem_prompt, rel_bias, g_ffn1, w1_gate, w1_up, w1_down, g_mix, w_in, conv_w, conv_b, dt_bias, a_log, d_skip, g_ssm, w_out, g_mem, w_ck, w_cv, g_cross, w_cq, w_co, g_ffn2, w2_gate, w2_up, w2_down, g_final):
    depth = g_ffn1.shape[0]
    b, s, d = x_prompt.shape
    bs, dec_seq, _ = x_sample.shape
    assert dec_seq == 1, "sample path handles one new token per sequence"
    n_heads_a, hd_a = cache_win_k.shape[-2:]
    d_attn = n_heads_a * hd_a
    n_heads_b, hd_b, d_state = state_ssm.shape[-3:]
    d_inner = n_heads_b * hd_b
    d_xbc = conv_w.shape[-1]
    n_groups = (d_xbc - d_inner) // (2 * d_state)
    n_mem, n_heads_x, hd_x = cache_mem_k.shape[-3:]
    w_buf = cache_win_k.shape[2]
    assert hd_a * 2 == LANES and hd_b * 2 == LANES and d_state == LANES
    assert s % (DILATIONS[-1] * N_BACK) == 0 and s >= CONV_WIDTH - 1

    cuts = (d_attn, 2 * d_attn, 3 * d_attn, 3 * d_attn + d_inner, 3 * d_attn + d_inner + d_xbc)
    d_in = w_in.shape[-1]
    d_in_pad = cuts[-1] + LANES
    n_pairs = d_inner // LANES
    xbc_rows = d_xbc // LANES

    def row(a):
        return a.reshape(1, -1)

    def per_feature(a):
        return jnp.repeat(a, hd_b).reshape(1, d_inner)

    def head_lanes(a):
        return jnp.pad(a, (0, LANES - n_heads_b)).reshape(1, LANES)

    bias_p, bias_s, count_s = _bias_tables(rel_bias, w_buf)

    yp = x_prompt.reshape(b * s, d)
    ysm = x_sample.reshape(bs, d)
    tm_p = _tile(b * s, 512)
    tm_s = _tile(bs, 512)
    gfin = row(g_final)
    outs = [[] for _ in range(10)]
    for i in range(depth):
        bf = lambda a: a[i].astype(BF16)
        w1g, w1u, w1d = bf(w1_gate), bf(w1_up), bf(w1_down)
        w2g, w2u, w2d = bf(w2_gate), bf(w2_up), bf(w2_down)
        w_in_p = jnp.pad(w_in[i], ((0, 0), (0, d_in_pad - d_in))).astype(BF16)
        woa, wos = w_out[i, :d_attn].astype(BF16), w_out[i, d_attn:].astype(BF16)
        wck, wcv, wcq, wco = bf(w_ck), bf(w_cv), bf(w_cq), bf(w_co)
        last = i == depth - 1

        x1 = _ffn(yp, row(g_ffn1[i]), w1g, w1u, w1d, gfin, final_norm=False, tm=tm_p)
        q, k, v, z, xbc, dtp, kt, vt = _inproj(x1, row(g_mix[i]), w_in_p, cuts=cuts, q_scale=hd_a ** -0.5,
                                               tm=_tile(s, 512), seq=s)
        to_seq = lambda a: a.reshape(b, s, a.shape[-1])
        oa = _attn_prompt(to_seq(q), to_seq(k), to_seq(v), bias_p, hd=hd_a)
        y_ssm, h_last = _ssd_prompt(
            to_seq(z), to_seq(xbc), to_seq(dtp), conv_w[i], row(conv_b[i]), head_lanes(dt_bias[i]),
            head_lanes(a_log[i]), per_feature(d_skip[i]), row(g_ssm[i]),
            n_heads=n_heads_b, hd=hd_b, n_groups=n_groups, d_state=d_state, ts=_tile(s, 512))
        mk, mv = _memkv(mem_prompt.reshape(b * n_mem, d), row(g_mem[i]), wck, wcv, tm=_tile(b * n_mem, 512))
        x3 = _post_prompt(to_seq(x1), oa, y_ssm, mk.reshape(b, n_mem, d), mv.reshape(b, n_mem, d),
                          woa, wos, row(g_cross[i]), wcq, wco, n_heads_x=n_heads_x, tm=_tile(s, 512))
        yp = _ffn(x3.reshape(b * s, d), row(g_ffn2[i]), w2g, w2u, w2d, gfin, final_norm=last, tm=tm_p)
        keep = min(MAX_DISTANCE, s)
        from_t = lambda a: a.reshape(b, n_heads_a, hd_a, s).transpose(0, 3, 1, 2)[:, s - keep:]
        outs[0].append(from_t(kt))
        outs[1].append(from_t(vt))
        outs[2].append(to_seq(xbc)[:, s - (CONV_WIDTH - 1):])
        outs[3].append(h_last.reshape(b, n_heads_b, hd_b, d_state))
        outs[4].append(mk.reshape(b, n_mem, n_heads_x, hd_x))
        outs[5].append(mv.reshape(b, n_mem, n_heads_x, hd_x))

        x1s = _ffn(ysm, row(g_ffn1[i]), w1g, w1u, w1d, gfin, final_norm=False, tm=tm_s)
        qs, ks, vs, zs, xbcs, dts = _inproj(x1s, row(g_mix[i]), w_in_p, cuts=cuts, q_scale=hd_a ** -0.5, tm=tm_s)
        r3 = lambda a: a.reshape(bs, 1, a.shape[-1])
        to_t = lambda c: c.transpose(0, 2, 3, 1).reshape(bs, d_attn, w_buf)
        oas = _attn_sample(r3(qs), r3(ks), r3(vs), to_t(cache_win_k[i]), to_t(cache_win_v[i]),
                           bias_s, count_s, hd=hd_a)
        tiles = lambda a: a.reshape(a.shape[:-1] + (a.shape[-1] // LANES, LANES))
        dt_feat = jnp.repeat(dts[:, :n_heads_b], hd_b, axis=-1)
        ys_t, nconv, nstate = _ssd_sample(
            tiles(zs), tiles(xbcs), tiles(dt_feat), tiles(cache_conv[i]),
            state_ssm[i].reshape(bs, d_inner, d_state),
            tiles(conv_w[i]), tiles(conv_b[i]), tiles(jnp.repeat(dt_bias[i], hd_b)),
            tiles(jnp.repeat(a_log[i], hd_b)), tiles(jnp.repeat(d_skip[i], hd_b)), tiles(g_ssm[i]),
            n_groups=n_groups, bb=_tile(bs, 8))
        x2s, qxs = _mix_out_sample(x1s, oas.reshape(bs, d_attn), ys_t.reshape(bs, d_inner), woa, wos,
                                   row(g_cross[i]), wcq, scale=hd_x ** -0.5)
        ocs = _cross_sample(qxs.reshape(bs, n_heads_x, hd_x), cache_mem_k[i], cache_mem_v[i],
                            bb=2 if bs % 2 == 0 else 1)
        x3s = _cross_out_sample(x2s, ocs.reshape(bs, d), wco)
        ysm = _ffn(x3s, row(g_ffn2[i]), w2g, w2u, w2d, gfin, final_norm=last, tm=tm_s)
        outs[6].append(ks.reshape(bs, 1, n_heads_a, hd_a))
        outs[7].append(vs.reshape(bs, 1, n_heads_a, hd_a))
        outs[8].append(nconv.reshape(bs, CONV_WIDTH - 1, d_xbc))
        outs[9].append(nstate.reshape(bs, n_heads_b, hd_b, d_state))

    return (yp.reshape(b, s, d), ysm.reshape(bs, 1, d)) + tuple(jnp.stack(o) for o in outs)
```

```python
import functools
import math

import jax
import jax.numpy as jnp
from jax import lax
from jax.experimental import pallas as pl
from jax.experimental.pallas import tpu as pltpu

F32 = jnp.float32
BF16 = jnp.bfloat16

EPS = 1e-6
NEG_INF = -1e30
DILATIONS = (1, 4, 16)
N_BACK = 128
UNROLL = 4
MAX_DISTANCE = 2048
SSD_CHUNK = 128
CONV_WIDTH = 4

LANES = 128
SUBLANES = 8
VMEM_LIMIT_BYTES = 56 * 1024 * 1024

_HI = lax.Precision.HIGHEST


def _cparams(*sem):
    return pltpu.CompilerParams(dimension_semantics=sem, vmem_limit_bytes=VMEM_LIMIT_BYTES)


def _const_spec(shape):
    nd = len(shape)
    return pl.BlockSpec(shape, lambda *_: (0,) * nd, pipeline_mode=pl.Buffered(1))


def _rms(x, g):
    return x * lax.rsqrt(jnp.mean(x * x, axis=-1, keepdims=True) + EPS) * g


def _silu(x):
    h = 0.5 * x
    return h + h * jnp.tanh(h)


def _softplus(x):
    return jnp.maximum(x, 0.0) + jnp.log1p(jnp.exp(-jnp.abs(x)))


def _dot(a, b):
    return jnp.dot(a, b, preferred_element_type=F32)


def _dot_nt(a, b):
    return lax.dot_general(a, b, (((1,), (1,)), ((), ())), preferred_element_type=F32)


def _dot_tn(a, b, precision=None):
    return lax.dot_general(a, b, (((0,), (0,)), ((), ())), preferred_element_type=F32,
                           precision=precision)


def _ffn_body(x_ref, g_ref, wg_ref, wu_ref, wd_ref, gf_ref, o_ref, *, final_norm):
    x = x_ref[...]
    xn = _rms(x, g_ref[...]).astype(BF16)
    gate = _dot(xn, wg_ref[...])
    up = _dot(xn, wu_ref[...])
    h = (_silu(gate) * up).astype(BF16)
    y = x + 0.5 * _dot(h, wd_ref[...])
    if final_norm:
        y = _rms(y, gf_ref[...])
    o_ref[...] = y


def _ffn(x, g, wg, wu, wd, g_final, *, final_norm, tm):
    n, d = x.shape
    ff = wg.shape[1]
    row = pl.BlockSpec((tm, d), lambda i: (i, 0))
    return pl.pallas_call(
        functools.partial(_ffn_body, final_norm=final_norm),
        grid=(n // tm,),
        in_specs=[row, _const_spec((1, d)), _const_spec((d, ff)), _const_spec((d, ff)),
                  _const_spec((ff, d)), _const_spec((1, d))],
        out_specs=row,
        out_shape=jax.ShapeDtypeStruct((n, d), F32),
        compiler_params=_cparams("parallel"),
        name="ffn",
    )(x, g, wg, wu, wd, g_final)


def _inproj_body(x_ref, g_ref, w_ref, q_ref, k_ref, v_ref, z_ref, xbc_ref, dt_ref, *, cuts, q_scale):
    xn = _rms(x_ref[...], g_ref[...]).astype(BF16)
    proj = _dot(xn, w_ref[...])
    c0, c1, c2, c3, c4 = cuts
    q_ref[...] = proj[:, :c0] * q_scale
    k_ref[...] = proj[:, c0:c1]
    v_ref[...] = proj[:, c1:c2]
    z_ref[...] = proj[:, c2:c3]
    xbc_ref[...] = proj[:, c3:c4]
    dt_ref[...] = proj[:, c4:]


def _inproj(x, g, w_pad, *, cuts, q_scale, tm):
    n, d = x.shape
    wp = w_pad.shape[1]
    c0, c1, c2, c3, c4 = cuts
    widths = (c0, c1 - c0, c2 - c1, c3 - c2, c4 - c3, wp - c4)
    row = lambda w: pl.BlockSpec((tm, w), lambda i: (i, 0))
    return pl.pallas_call(
        functools.partial(_inproj_body, cuts=cuts, q_scale=q_scale),
        grid=(n // tm,),
        in_specs=[row(d), _const_spec((1, d)), _const_spec((d, wp))],
        out_specs=[row(w) for w in widths],
        out_shape=[jax.ShapeDtypeStruct((n, w), F32) for w in widths],
        compiler_params=_cparams("parallel"),
        name="inproj",
    )(x, g, w_pad)


def _inproj_prompt_body(x_ref, g_ref, w_ref, cw_ref, cb_ref, dtb_ref,
                        q_ref, k_ref, v_ref, gate_ref, xbc_ref, dt_ref, kt_ref, vt_ref, tail_ref,
                        xpad_ref, *, cuts, q_scale, per_seq):
    tm = x_ref.shape[0]
    n_slabs = xpad_ref.shape[0]
    taps = cw_ref.shape[0]
    xn = _rms(x_ref[...], g_ref[...]).astype(BF16)
    c0, c1, c2, c3, c4 = cuts

    @pl.when(pl.program_id(0) % per_seq == 0)
    def _():
        xpad_ref[:, 0:SUBLANES, :] = jnp.zeros((n_slabs, SUBLANES, LANES), F32)

    raw = _dot(xn, w_ref[:, c3:c4])
    gate_ref[...] = _silu(_dot(xn, w_ref[:, c2:c3]))
    qkv = _dot(xn, w_ref[:, :c2])
    q_ref[...] = qkv[:, :c0] * q_scale
    k_ref[...] = qkv[:, c0:c1]
    v_ref[...] = qkv[:, c1:c2]
    kt_ref[...] = qkv[:, c0:c1].T
    vt_ref[...] = qkv[:, c1:c2].T
    dt_ref[...] = _softplus(_dot(xn, w_ref[:, c4:]) + dtb_ref[...])
    for j in range(n_slabs):
        xpad_ref[j, pl.ds(SUBLANES, tm), :] = raw[:, j * LANES:(j + 1) * LANES]
    cw = cw_ref[...]
    for j in range(n_slabs):
        sl = slice(j * LANES, (j + 1) * LANES)
        acc = cb_ref[:, sl]
        for k in range(taps):
            acc = acc + xpad_ref[j, pl.ds(SUBLANES - (taps - 1) + k, tm, stride=1), :] * cw[k:k + 1, sl]
        xbc_ref[:, sl] = _silu(acc)
    tail_ref[...] = raw[tm - SUBLANES:, :]
    xpad_ref[:, 0:SUBLANES, :] = xpad_ref[:, pl.ds(tm, SUBLANES), :]


def _inproj_prompt(x, g, w_pad, cw, cb, dtb, *, cuts, q_scale, tm, seq):
    n, d = x.shape
    wp = w_pad.shape[1]
    c0, c1, c2, c3, c4 = cuts
    widths = (c0, c1 - c0, c2 - c1, c3 - c2, c4 - c3, wp - c4)
    d_xbc = c4 - c3
    per_seq = seq // tm
    row = lambda w: pl.BlockSpec((tm, w), lambda i: (i, 0))
    feat = lambda w: pl.BlockSpec((None, w, tm), lambda i: (i // per_seq, 0, i % per_seq))
    out_specs = [row(w) for w in widths] + [feat(widths[1]), feat(widths[2]),
                                            pl.BlockSpec((None, SUBLANES, d_xbc), lambda i: (i // per_seq, 0, 0))]
    out_shape = [jax.ShapeDtypeStruct((n, w), F32) for w in widths] + [
        jax.ShapeDtypeStruct((n // seq, widths[1], seq), F32),
        jax.ShapeDtypeStruct((n // seq, widths[2], seq), F32),
        jax.ShapeDtypeStruct((n // seq, SUBLANES, d_xbc), F32)]
    return pl.pallas_call(
        functools.partial(_inproj_prompt_body, cuts=cuts, q_scale=q_scale, per_seq=per_seq),
        grid=(n // tm,),
        in_specs=[row(d), _const_spec((1, d)), _const_spec((d, wp)),
                  _const_spec(cw.shape), _const_spec(cb.shape), _const_spec(dtb.shape)],
        out_specs=out_specs,
        out_shape=out_shape,
        scratch_shapes=[pltpu.VMEM((d_xbc // LANES, SUBLANES + tm, LANES), F32)],
        compiler_params=_cparams("arbitrary"),
        name="inproj_prompt",
    )(x, g, w_pad, cw, cb, dtb)


def _t5_bucket(dist, num_buckets):
    max_exact = num_buckets // 2
    d = jnp.maximum(dist, 0)
    df = jnp.maximum(d, 1).astype(F32)
    large = max_exact + (jnp.log(df / max_exact) / math.log(MAX_DISTANCE / max_exact)
                         * (num_buckets - max_exact)).astype(jnp.int32)
    large = jnp.minimum(large, num_buckets - 1)
    return jnp.where(d < max_exact, d, large)


def _bias_body(rb_ref, idxp_ref, idxs_ref, cnt_ref, bp_ref, bs_ref, *, n_buckets, n_heads):
    n_pat = idxp_ref.shape[0]
    blk = idxp_ref.shape[1]
    qi = lax.broadcasted_iota(jnp.int32, (blk, 2 * blk), 0)
    kj = lax.broadcasted_iota(jnp.int32, (blk, 2 * blk), 1)
    delta = qi + blk - kj
    band = (delta >= 0) & (delta <= N_BACK)

    def lookup(idx, h):
        acc = jnp.zeros(idx.shape, F32)
        for c in range(n_buckets):
            acc = jnp.where(idx == c, rb_ref[c, h], acc)
        return acc

    for h in range(n_heads):
        for p in range(n_pat):
            bp_ref[p, h] = jnp.where(band, lookup(idxp_ref[p], h), NEG_INF)
        bs_ref[h:h + 1, :] = jnp.where(cnt_ref[...] > 0.0, lookup(idxs_ref[...], h), NEG_INF)


def _bias_tables(rel_bias, w_buf):
    n_buckets, n_heads = rel_bias.shape
    blk = N_BACK
    n_pat = len(DILATIONS)
    qi = jnp.arange(blk)[:, None]
    kj = jnp.arange(2 * blk)[None, :]
    delta = qi + blk - kj
    idx_p = jnp.stack([_t5_bucket(delta * d, n_buckets) for d in DILATIONS]).astype(jnp.int32)
    dist = jnp.concatenate([w_buf - jnp.arange(w_buf), jnp.zeros((LANES,), jnp.int32)])
    idx_s = _t5_bucket(dist, n_buckets).astype(jnp.int32)[None, :]
    count = sum(((dist % d == 0) & (dist <= d * N_BACK)).astype(F32) for d in DILATIONS)[None, :]
    vm = pl.BlockSpec(memory_space=pltpu.VMEM)
    bias_p, bias_s = pl.pallas_call(
        functools.partial(_bias_body, n_buckets=n_buckets, n_heads=n_heads),
        in_specs=[pl.BlockSpec(memory_space=pltpu.SMEM), vm, vm, vm],
        out_specs=[vm, vm],
        out_shape=[jax.ShapeDtypeStruct((n_pat, n_heads, blk, 2 * blk), F32),
                   jax.ShapeDtypeStruct((n_heads, w_buf + LANES), F32)],
        name="bias_tables",
    )(rel_bias, idx_p, idx_s, count)
    return bias_p, bias_s, count


def _attn_prompt_body(q_ref, k_ref, v_ref, bias_ref, o_ref, acc2, l2, m2, acc3, l3, m3, *, hd):
    seq = q_ref.shape[0]
    blk = N_BACK
    lane = lax.broadcasted_iota(jnp.int32, (1, LANES), 1)
    head_mask = (lane < hd, lane >= hd)

    def block(qb, kb, vb, bias2):
        nq = qb.shape[0]
        q2 = jnp.concatenate([jnp.where(head_mask[h], qb, 0.0).astype(BF16) for h in range(2)], axis=0)
        s = _dot_nt(q2, kb.astype(BF16)) + bias2
        m = jnp.max(s, axis=-1, keepdims=True)
        p = jnp.exp(s - m).astype(BF16)
        v2 = jnp.concatenate(
            [jnp.concatenate([jnp.where(head_mask[h], vb, 0.0).astype(BF16),
                              jnp.broadcast_to(jnp.where(head_mask[h], 1.0, 0.0).astype(BF16), vb.shape)], axis=1)
             for h in range(2)], axis=0)
        out = _dot(jnp.concatenate([p[:nq], p[nq:]], axis=1), v2)
        m_full = jnp.where(head_mask[0], m[:nq], m[nq:])
        return out[:, :LANES], out[:, LANES:], m_full

    def cur_bias(p):
        return bias_ref[p, :, :, blk:].reshape(2 * blk, blk)

    def full_bias(p):
        return bias_ref[p].reshape(2 * blk, 2 * blk)

    d3 = DILATIONS[2]

    def pat3(g, carry):
        for u in range(UNROLL):
            rows = pl.ds(g * UNROLL + u, blk, stride=d3)
            a, l, m = block(q_ref[rows, :], k_ref[rows, :], v_ref[rows, :], cur_bias(2))
            acc3[rows, :] = a
            l3[rows, :] = l
            m3[rows, :] = m
        return carry

    lax.fori_loop(0, d3 // UNROLL, pat3, 0)

    d2 = DILATIONS[1]
    nb2 = seq // (d2 * blk)

    def pat2(r, carry):
        for i in range(nb2):
            rows = pl.ds(r + d2 * blk * i, blk, stride=d2)
            if i == 0:
                krows, bias = rows, cur_bias(1)
            else:
                krows, bias = pl.ds(r + d2 * blk * (i - 1), 2 * blk, stride=d2), full_bias(1)
            a, l, m = block(q_ref[rows, :], k_ref[krows, :], v_ref[krows, :], bias)
            acc2[rows, :] = a
            l2[rows, :] = l
            m2[rows, :] = m
        return carry

    lax.fori_loop(0, d2, pat2, 0)

    def finish(rows, a1, l1, m1):
        a2_, l2_, m2_ = acc2[rows, :], l2[rows, :], m2[rows, :]
        a3_, l3_, m3_ = acc3[rows, :], l3[rows, :], m3[rows, :]
        mx = jnp.maximum(jnp.maximum(m1, m2_), m3_)
        w1, w2, w3 = jnp.exp(m1 - mx), jnp.exp(m2_ - mx), jnp.exp(m3_ - mx)
        num = w1 * a1 + w2 * a2_ + w3 * a3_
        den = w1 * l1 + w2 * l2_ + w3 * l3_
        o_ref[rows, :] = (num / den).astype(o_ref.dtype)

    def pat1_block(start):
        rows = pl.ds(start, blk)
        kstart = start - blk if isinstance(start, int) else pl.multiple_of(start - blk, blk)
        krows = pl.ds(kstart, 2 * blk)
        finish(rows, *block(q_ref[rows, :], k_ref[krows, :], v_ref[krows, :], full_bias(0)))

    rows0 = pl.ds(0, blk)
    finish(rows0, *block(q_ref[rows0, :], k_ref[rows0, :], v_ref[rows0, :], cur_bias(0)))
    for u in range(1, UNROLL):
        pat1_block(u * blk)

    def pat1(g, carry):
        for u in range(UNROLL):
            pat1_block(pl.multiple_of((g * UNROLL + u) * blk, blk))
        return carry

    lax.fori_loop(1, seq // (blk * UNROLL), pat1, 0)


def _attn_prompt(q, k, v, bias_p, *, hd):
    b, s, da = q.shape
    n_pairs = da // LANES
    n_pat = bias_p.shape[0]
    blk = N_BACK
    tok = pl.BlockSpec((None, s, LANES), lambda i, j: (i, 0, j))
    return pl.pallas_call(
        functools.partial(_attn_prompt_body, hd=hd),
        grid=(b, n_pairs),
        in_specs=[tok, tok, tok,
                  pl.BlockSpec((n_pat, 2, blk, 2 * blk), lambda i, j: (0, j, 0, 0))],
        out_specs=tok,
        out_shape=jax.ShapeDtypeStruct((b, s, da), BF16),
        scratch_shapes=[pltpu.VMEM((s, LANES), F32) for _ in range(6)],
        compiler_params=_cparams("parallel", "arbitrary"),
        name="attn_prompt",
    )(q, k, v, bias_p)


def _ssd_prompt_body(gate_ref, xbc_ref, dt_ref, alog_ref, dskip_ref, gssm_ref,
                     y_ref, hout_ref, ht_ref, *, n_heads, hd, n_groups, d_state):
    step = pl.program_id(1)
    ts = gate_ref.shape[0]
    q = SSD_CHUNK
    d_inner = n_heads * hd
    gw = d_inner // n_groups
    pairs_per_group = gw // LANES

    @pl.when(step == 0)
    def _():
        ht_ref[...] = jnp.zeros_like(ht_ref)

    lane = lax.broadcasted_iota(jnp.int32, (1, LANES), 1)
    head_mask = (lane < hd, lane >= hd)
    row_q = lax.broadcasted_iota(jnp.int32, (q, q), 0)
    col_q = lax.broadcasted_iota(jnp.int32, (q, q), 1)
    causal = row_q >= col_q
    tri = jnp.where(causal, 1.0, 0.0).astype(BF16)

    neg_a_log2e = -jnp.exp(alog_ref[...]) * math.log2(math.e)

    def chunk(c, carry):
        t0 = pl.multiple_of(c * q, q)
        rows = pl.ds(t0, q)
        xs = xbc_ref[rows, :d_inner]
        bm = xbc_ref[rows, d_inner:d_inner + n_groups * d_state]
        cm = xbc_ref[rows, d_inner + n_groups * d_state:]

        dt = dt_ref[rows, :]
        la = dt * neg_a_log2e
        la_hi = la.astype(BF16)
        r1 = la - la_hi.astype(F32)
        la_mid = r1.astype(BF16)
        la_lo = (r1 - la_mid.astype(F32)).astype(BF16)
        a_cs = _dot(tri, la_hi) + _dot(tri, la_mid) + _dot(tri, la_lo)
        total = a_cs[q - 1:q, :]
        w_s = dt * jnp.exp2(total - a_cs)
        chunk_decay = jnp.exp2(total)
        a_cs_t, dt_t, w_t = a_cs.T, dt.T, w_s.T

        y_tiles = []
        for g in range(n_groups):
            b_g = bm[:, g * d_state:(g + 1) * d_state]
            c_g = cm[:, g * d_state:(g + 1) * d_state]
            cb = _dot_nt(c_g.astype(BF16), b_g.astype(BF16))
            b_t = b_g.T
            for jl in range(pairs_per_group):
                j = g * pairs_per_group + jl
                pair = slice(jl * LANES, (jl + 1) * LANES)
                xs_pair = xs[:, j * LANES:(j + 1) * LANES]
                ht_pair = ht_ref[g, :, pair]
                y_pair = None
                st_pair = None
                cd_pair = None
                for h in range(2):
                    e = 2 * j + h
                    col = jnp.broadcast_to(a_cs[:, e:e + 1], (q, q))
                    lmat = jnp.exp2(jnp.where(causal, col - a_cs_t[e:e + 1, :], NEG_INF))
                    col_n = col if d_state == q else jnp.broadcast_to(a_cs[:, e:e + 1], (q, d_state))
                    lhs = jnp.concatenate([(cb * lmat * dt_t[e:e + 1, :]).astype(BF16),
                                           (c_g * jnp.exp2(col_n)).astype(BF16)], axis=1)
                    xs_h = jnp.where(head_mask[h], xs_pair, 0.0).astype(BF16)
                    rhs = jnp.concatenate([xs_h, jnp.where(head_mask[h], ht_pair, 0.0).astype(BF16)], axis=0)
                    yh = _dot(lhs, rhs)
                    sth = _dot((b_t * w_t[e:e + 1, :]).astype(BF16), xs_h)
                    cdh = chunk_decay[:, e:e + 1]
                    y_pair = yh if y_pair is None else y_pair + yh
                    st_pair = sth if st_pair is None else st_pair + sth
                    cd_pair = cdh if cd_pair is None else jnp.where(head_mask[0], cd_pair, cdh)
                y_tiles.append(y_pair + dskip_ref[:, j * LANES:(j + 1) * LANES] * xs_pair)
                ht_ref[g, :, pair] = ht_pair * cd_pair + st_pair

        y = jnp.concatenate(y_tiles, axis=1)
        yg = y * gate_ref[rows, :]
        parts = []
        for g in range(n_groups):
            part = yg[:, g * gw:(g + 1) * gw]
            parts.append(part * lax.rsqrt(jnp.mean(part * part, axis=-1, keepdims=True) + EPS))
        y_ref[rows, :] = (jnp.concatenate(parts, axis=1) * gssm_ref[...]).astype(y_ref.dtype)
        return carry

    lax.fori_loop(0, ts // q, chunk, 0)

    @pl.when(step == pl.num_programs(1) - 1)
    def _():
        for g in range(n_groups):
            hout_ref[g * gw:(g + 1) * gw, :] = ht_ref[g].T


def _ssd_prompt(gate, xbc, dt, alog, dskip, gssm, *, n_heads, hd, n_groups, d_state, ts):
    b, s, d_inner = gate.shape
    d_xbc = xbc.shape[-1]
    gw = d_inner // n_groups
    tok = lambda w: pl.BlockSpec((None, ts, w), lambda i, j: (i, j, 0))
    return pl.pallas_call(
        functools.partial(_ssd_prompt_body, n_heads=n_heads, hd=hd, n_groups=n_groups, d_state=d_state),
        grid=(b, s // ts),
        in_specs=[tok(d_inner), tok(d_xbc), tok(LANES),
                  _const_spec(alog.shape), _const_spec(dskip.shape), _const_spec(gssm.shape)],
        out_specs=[tok(d_inner), pl.BlockSpec((None, d_inner, d_state), lambda i, j: (i, 0, 0))],
        out_shape=[jax.ShapeDtypeStruct((b, s, d_inner), BF16),
                   jax.ShapeDtypeStruct((b, d_inner, d_state), F32)],
        scratch_shapes=[pltpu.VMEM((n_groups, d_state, gw), F32)],
        compiler_params=_cparams("parallel", "arbitrary"),
        name="ssd_prompt",
    )(gate, xbc, dt, alog, dskip, gssm)


def _memkv_body(m_ref, g_ref, wk_ref, wv_ref, k_ref, v_ref):
    mn = _rms(m_ref[...], g_ref[...]).astype(BF16)
    k_ref[...] = _dot(mn, wk_ref[...])
    v_ref[...] = _dot(mn, wv_ref[...])


def _memkv(mem, g, wk, wv, *, tm):
    n, d = mem.shape
    dk = wk.shape[1]
    row = lambda w: pl.BlockSpec((tm, w), lambda i: (i, 0))
    return pl.pallas_call(
        _memkv_body,
        grid=(n // tm,),
        in_specs=[row(d), _const_spec((1, d)), _const_spec(wk.shape), _const_spec(wv.shape)],
        out_specs=[row(dk), row(dk)],
        out_shape=[jax.ShapeDtypeStruct((n, dk), F32)] * 2,
        compiler_params=_cparams("parallel"),
        name="memkv",
    )(mem, g, wk, wv)


def _post_prompt_body(x_ref, oa_ref, ys_ref, mk_ref, mv_ref, woa_ref, wos_ref, gc_ref, wcq_ref, wco_ref,
                      o_ref, *, n_heads_x):
    x2 = x_ref[...] + _dot(oa_ref[...], woa_ref[...]) + _dot(ys_ref[...], wos_ref[...])
    dx = x2.shape[1] // n_heads_x
    hq = _rms(x2, gc_ref[...]).astype(BF16)
    qx = (_dot(hq, wcq_ref[...]) * (dx ** -0.5)).astype(BF16)
    mk = mk_ref[...].astype(BF16)
    mv = mv_ref[...].astype(BF16)
    outs = []
    for h in range(n_heads_x):
        sl = slice(h * dx, (h + 1) * dx)
        s = _dot_nt(qx[:, sl], mk[:, sl])
        p = jnp.exp(s - jnp.max(s, axis=-1, keepdims=True))
        l = jnp.sum(p, axis=-1, keepdims=True)
        outs.append(_dot(p.astype(BF16), mv[:, sl]) / l)
    oc = jnp.concatenate(outs, axis=1).astype(BF16)
    o_ref[...] = x2 + _dot(oc, wco_ref[...])


def _post_prompt(x1, oa, ys, mk, mv, woa, wos, gc, wcq, wco, *, n_heads_x, tm):
    b, s, d = x1.shape
    n_mem = mk.shape[1]
    tok = lambda w: pl.BlockSpec((None, tm, w), lambda i, j: (i, j, 0))
    mem = pl.BlockSpec((None, n_mem, d), lambda i, j: (i, 0, 0))
    return pl.pallas_call(
        functools.partial(_post_prompt_body, n_heads_x=n_heads_x),
        grid=(b, s // tm),
        in_specs=[tok(d), tok(oa.shape[-1]), tok(ys.shape[-1]), mem, mem,
                  _const_spec(woa.shape), _const_spec(wos.shape), _const_spec(gc.shape),
                  _const_spec(wcq.shape), _const_spec(wco.shape)],
        out_specs=tok(d),
        out_shape=jax.ShapeDtypeStruct((b, s, d), F32),
        compiler_params=_cparams("parallel", "arbitrary"),
        name="post_prompt",
    )(x1, oa, ys, mk, mv, woa, wos, gc, wcq, wco)


def _attn_sample_body(q_ref, kn_ref, vn_ref, kt_ref, vt_ref, bias_ref, cnt_ref, o_ref, *, hd):
    da, w = kt_ref.shape
    n_heads = da // hd
    row = lax.broadcasted_iota(jnp.int32, (n_heads, da), 0)
    col = lax.broadcasted_iota(jnp.int32, (n_heads, da), 1)
    hmask = (col >= row * hd) & (col < (row + 1) * hd)
    qbd16 = jnp.where(hmask, q_ref[...], 0.0).astype(BF16)
    s = _dot(qbd16, kt_ref[...].astype(BF16)) + bias_ref[:, :w]
    kn = kn_ref[...].astype(BF16).astype(F32)
    s_new = jnp.sum(qbd16.astype(F32) * kn, axis=-1, keepdims=True) + bias_ref[:, w:w + 1]
    m = jnp.maximum(jnp.max(s, axis=-1, keepdims=True), s_new)
    e = jnp.exp(s - m) * cnt_ref[:, :w]
    e_new = jnp.exp(s_new - m) * cnt_ref[:, w:w + 1]
    den = jnp.sum(e, axis=-1, keepdims=True) + e_new
    acc = _dot_nt(e.astype(BF16), vt_ref[...].astype(BF16)) + e_new * vn_ref[...]
    o_ref[...] = jnp.sum(jnp.where(hmask, acc / den, 0.0), axis=0, keepdims=True)


def _attn_sample(q, kn, vn, kct, vct, bias_s, count, *, hd):
    bs, _, da = q.shape
    w = kct.shape[-1]
    row = pl.BlockSpec((None, 1, da), lambda i: (i, 0, 0))
    cache = pl.BlockSpec((None, da, w), lambda i: (i, 0, 0))
    return pl.pallas_call(
        functools.partial(_attn_sample_body, hd=hd),
        grid=(bs,),
        in_specs=[row, row, row, cache, cache, _const_spec(bias_s.shape), _const_spec(count.shape)],
        out_specs=row,
        out_shape=jax.ShapeDtypeStruct((bs, 1, da), F32),
        compiler_params=_cparams("parallel"),
        name="attn_sample",
    )(q, kn, vn, kct, vct, bias_s, count)


def _ssd_sample_body(z_ref, xbc_ref, dt_ref, cc_ref, st_ref, cw_ref, cb_ref, dtb_ref, alog_ref, dskip_ref,
                     gssm_ref, y_ref, nconv_ref, nst_ref, *, n_groups):
    bb = z_ref.shape[0]
    n_pairs = z_ref.shape[1]
    pairs_per_group = n_pairs // n_groups
    gw = pairs_per_group * LANES
    row8 = lax.broadcasted_iota(jnp.int32, (n_pairs, 1), 0)
    neg_a = -jnp.exp(alog_ref[...])
    pad = jnp.zeros((LANES - 2 * n_pairs, LANES), F32)

    def one(b, carry):
        cc = cc_ref[b]
        xr = xbc_ref[b]
        pre = xr * cw_ref[CONV_WIDTH - 1] + cb_ref[...]
        for k in range(CONV_WIDTH - 1):
            pre = pre + cc[k] * cw_ref[k]
            nconv_ref[b, k] = cc[k + 1] if k + 1 < CONV_WIDTH - 1 else xr
        act = _silu(pre)
        xs = act[:n_pairs]
        bm = act[n_pairs:n_pairs + n_groups]
        cm = act[n_pairs + n_groups:]
        dt = _softplus(dt_ref[b] + dtb_ref[...])
        d_a = jnp.exp(dt * neg_a)
        cols = jnp.concatenate([xs * dt, d_a, pad], axis=0).T
        y = jnp.zeros((n_pairs, LANES), F32)
        for j in range(n_pairs):
            g = j // pairs_per_group
            rows = pl.ds(j * LANES, LANES)
            h_new = cols[:, n_pairs + j:n_pairs + j + 1] * st_ref[b, rows, :] \
                + cols[:, j:j + 1] * bm[g:g + 1, :]
            nst_ref[b, rows, :] = h_new
            c_sel = jnp.where(row8 == j, cm[g:g + 1, :], 0.0).astype(BF16)
            y = y + _dot_nt(c_sel, h_new.astype(BF16))
        y = y + dskip_ref[...] * xs
        yg = y * _silu(z_ref[b])
        ss = jnp.sum(yg * yg, axis=-1, keepdims=True)
        mean = jnp.zeros_like(ss)
        for g in range(n_groups):
            in_g = (row8 >= g * pairs_per_group) & (row8 < (g + 1) * pairs_per_group)
            tot = jnp.sum(jnp.where(in_g, ss, 0.0), axis=0, keepdims=True)
            mean = jnp.where(in_g, tot / gw, mean)
        y_ref[b] = yg * lax.rsqrt(mean + EPS) * gssm_ref[...]
        return carry

    lax.fori_loop(0, bb, one, 0)


def _ssd_sample(z, xbc, dt, cc, st, cw, cb, dtb, alog, dskip, gssm, *, n_groups, bb):
    bs = z.shape[0]
    lead = lambda a: pl.BlockSpec((bb,) + a.shape[1:], lambda i: (i,) + (0,) * (a.ndim - 1))
    return pl.pallas_call(
        functools.partial(_ssd_sample_body, n_groups=n_groups),
        grid=(bs // bb,),
        in_specs=[lead(z), lead(xbc), lead(dt), lead(cc), lead(st)]
        + [_const_spec(a.shape) for a in (cw, cb, dtb, alog, dskip, gssm)],
        out_specs=[lead(z), lead(cc), lead(st)],
        out_shape=[jax.ShapeDtypeStruct(z.shape, F32), jax.ShapeDtypeStruct(cc.shape, F32),
                   jax.ShapeDtypeStruct(st.shape, F32)],
        compiler_params=_cparams("parallel"),
        name="ssd_sample",
    )(z, xbc, dt, cc, st, cw, cb, dtb, alog, dskip, gssm)


def _mix_out_sample_body(x_ref, oa_ref, ys_ref, woa_ref, wos_ref, gc_ref, wcq_ref, x2_ref, qx_ref, *, scale):
    x2 = x_ref[...] + _dot(oa_ref[...].astype(BF16), woa_ref[...]) + _dot(ys_ref[...].astype(BF16), wos_ref[...])
    x2_ref[...] = x2
    qx_ref[...] = _dot(_rms(x2, gc_ref[...]).astype(BF16), wcq_ref[...]) * scale


def _mix_out_sample(x1, oa, ys, woa, wos, gc, wcq, *, scale):
    vm = pl.BlockSpec(memory_space=pltpu.VMEM)
    return pl.pallas_call(
        functools.partial(_mix_out_sample_body, scale=scale),
        in_specs=[vm] * 7,
        out_specs=[vm, vm],
        out_shape=[jax.ShapeDtypeStruct(x1.shape, F32), jax.ShapeDtypeStruct((x1.shape[0], wcq.shape[1]), F32)],
        compiler_params=pltpu.CompilerParams(vmem_limit_bytes=VMEM_LIMIT_BYTES),
        name="mix_out_sample",
    )(x1, oa, ys, woa, wos, gc, wcq)


def _cross_sample_body(q_ref, mk_ref, mv_ref, o_ref):
    bb = q_ref.shape[0]

    def one(b, carry):
        s = jnp.sum(mk_ref[b] * q_ref[b][None], axis=-1, keepdims=True)
        p = jnp.exp(s - jnp.max(s, axis=0, keepdims=True))
        l = jnp.sum(p, axis=0)
        o_ref[b] = jnp.sum(p * mv_ref[b], axis=0) / l
        return carry

    lax.fori_loop(0, bb, one, 0)


def _cross_sample(qx, mk, mv, *, bb):
    bs, n_mem, nh, dx = mk.shape
    row = pl.BlockSpec((bb, nh, dx), lambda i: (i, 0, 0))
    mem = pl.BlockSpec((bb, n_mem, nh, dx), lambda i: (i, 0, 0, 0))
    return pl.pallas_call(
        _cross_sample_body,
        grid=(bs // bb,),
        in_specs=[row, mem, mem],
        out_specs=row,
        out_shape=jax.ShapeDtypeStruct((bs, nh, dx), F32),
        compiler_params=_cparams("parallel"),
        name="cross_sample",
    )(qx, mk, mv)


def _cross_out_sample_body(x_ref, oc_ref, wco_ref, o_ref):
    o_ref[...] = x_ref[...] + _dot(oc_ref[...].astype(BF16), wco_ref[...])


def _cross_out_sample(x2, oc, wco):
    vm = pl.BlockSpec(memory_space=pltpu.VMEM)
    return pl.pallas_call(
        _cross_out_sample_body,
        in_specs=[vm] * 3,
        out_specs=vm,
        out_shape=jax.ShapeDtypeStruct(x2.shape, F32),
        name="cross_out_sample",
    )(x2, oc, wco)


def _tile(n, target):
    t = min(n, target)
    while n % t or (t % SUBLANES and t != n):
        t -= 1
    return t


def kernel(x_prompt, x_sample, cache_win_k, cache_win_v, cache_conv, state_ssm, cache_mem_k, cache_mem_v, mem_prompt, rel_bias, g_ffn1, w1_gate, w1_up, w1_down, g_mix, w_in, conv_w, conv_b, dt_bias, a_log, d_skip, g_ssm, w_out, g_mem, w_ck, w_cv, g_cross, w_cq, w_co, g_ffn2, w2_gate, w2_up, w2_down, g_final):
    depth = g_ffn1.shape[0]
    b, s, d = x_prompt.shape
    bs, dec_seq, _ = x_sample.shape
    assert dec_seq == 1, "sample path handles one new token per sequence"
    n_heads_a, hd_a = cache_win_k.shape[-2:]
    d_attn = n_heads_a * hd_a
    n_heads_b, hd_b, d_state = state_ssm.shape[-3:]
    d_inner = n_heads_b * hd_b
    d_xbc = conv_w.shape[-1]
    n_groups = (d_xbc - d_inner) // (2 * d_state)
    n_mem, n_heads_x, hd_x = cache_mem_k.shape[-3:]
    w_buf = cache_win_k.shape[2]
    assert hd_a * 2 == LANES and hd_b * 2 == LANES and d_state == LANES
    assert s % (DILATIONS[-1] * N_BACK) == 0 and s >= CONV_WIDTH - 1

    cuts = (d_attn, 2 * d_attn, 3 * d_attn, 3 * d_attn + d_inner, 3 * d_attn + d_inner + d_xbc)
    d_in = w_in.shape[-1]
    d_in_pad = cuts[-1] + LANES
    n_pairs = d_inner // LANES
    xbc_rows = d_xbc // LANES

    def row(a):
        return a.reshape(1, -1)

    def per_feature(a):
        return jnp.repeat(a, hd_b).reshape(1, d_inner)

    def head_lanes(a):
        return jnp.pad(a, (0, LANES - n_heads_b)).reshape(1, LANES)

    bias_p, bias_s, count_s = _bias_tables(rel_bias, w_buf)

    yp = x_prompt.reshape(b * s, d)
    ysm = x_sample.reshape(bs, d)
    tm_p = _tile(b * s, 512)
    tm_s = _tile(bs, 512)
    gfin = row(g_final)
    outs = [[] for _ in range(10)]
    for i in range(depth):
        bf = lambda a: a[i].astype(BF16)
        w1g, w1u, w1d = bf(w1_gate), bf(w1_up), bf(w1_down)
        w2g, w2u, w2d = bf(w2_gate), bf(w2_up), bf(w2_down)
        w_in_p = jnp.pad(w_in[i], ((0, 0), (0, d_in_pad - d_in))).astype(BF16)
        woa, wos = w_out[i, :d_attn].astype(BF16), w_out[i, d_attn:].astype(BF16)
        wck, wcv, wcq, wco = bf(w_ck), bf(w_cv), bf(w_cq), bf(w_co)
        last = i == depth - 1

        x1 = _ffn(yp, row(g_ffn1[i]), w1g, w1u, w1d, gfin, final_norm=False, tm=tm_p)
        q, k, v, gate, xbc, dtp, kt, vt, xbc_tail = _inproj_prompt(
            x1, row(g_mix[i]), w_in_p, conv_w[i], row(conv_b[i]), head_lanes(dt_bias[i]),
            cuts=cuts, q_scale=hd_a ** -0.5, tm=_tile(s, 512), seq=s)
        to_seq = lambda a: a.reshape(b, s, a.shape[-1])
        oa = _attn_prompt(to_seq(q), to_seq(k), to_seq(v), bias_p, hd=hd_a)
        y_ssm, h_last = _ssd_prompt(
            to_seq(gate), to_seq(xbc), to_seq(dtp), head_lanes(a_log[i]), per_feature(d_skip[i]), row(g_ssm[i]),
            n_heads=n_heads_b, hd=hd_b, n_groups=n_groups, d_state=d_state, ts=_tile(s, 512))
        mk, mv = _memkv(mem_prompt.reshape(b * n_mem, d), row(g_mem[i]), wck, wcv, tm=_tile(b * n_mem, 512))
        x3 = _post_prompt(to_seq(x1), oa, y_ssm, mk.reshape(b, n_mem, d), mv.reshape(b, n_mem, d),
                          woa, wos, row(g_cross[i]), wcq, wco, n_heads_x=n_heads_x, tm=_tile(s, 512))
        yp = _ffn(x3.reshape(b * s, d), row(g_ffn2[i]), w2g, w2u, w2d, gfin, final_norm=last, tm=tm_p)
        keep = min(MAX_DISTANCE, s)
        from_t = lambda a: a.reshape(b, n_heads_a, hd_a, s).transpose(0, 3, 1, 2)[:, s - keep:]
        outs[0].append(from_t(kt))
        outs[1].append(from_t(vt))
        outs[2].append(xbc_tail[:, SUBLANES - (CONV_WIDTH - 1):])
        outs[3].append(h_last.reshape(b, n_heads_b, hd_b, d_state))
        outs[4].append(mk.reshape(b, n_mem, n_heads_x, hd_x))
        outs[5].append(mv.reshape(b, n_mem, n_heads_x, hd_x))

        x1s = _ffn(ysm, row(g_ffn1[i]), w1g, w1u, w1d, gfin, final_norm=False, tm=tm_s)
        qs, ks, vs, zs, xbcs, dts = _inproj(x1s, row(g_mix[i]), w_in_p, cuts=cuts, q_scale=hd_a ** -0.5, tm=tm_s)
        r3 = lambda a: a.reshape(bs, 1, a.shape[-1])
        to_t = lambda c: c.transpose(0, 2, 3, 1).reshape(bs, d_attn, w_buf)
        oas = _attn_sample(r3(qs), r3(ks), r3(vs), to_t(cache_win_k[i]), to_t(cache_win_v[i]),
                           bias_s, count_s, hd=hd_a)
        tiles = lambda a: a.reshape(a.shape[:-1] + (a.shape[-1] // LANES, LANES))
        dt_feat = jnp.repeat(dts[:, :n_heads_b], hd_b, axis=-1)
        ys_t, nconv, nstate = _ssd_sample(
            tiles(zs), tiles(xbcs), tiles(dt_feat), tiles(cache_conv[i]),
            state_ssm[i].reshape(bs, d_inner, d_state),
            tiles(conv_w[i]), tiles(conv_b[i]), tiles(jnp.repeat(dt_bias[i], hd_b)),
            tiles(jnp.repeat(a_log[i], hd_b)), tiles(jnp.repeat(d_skip[i], hd_b)), tiles(g_ssm[i]),
            n_groups=n_groups, bb=_tile(bs, 8))
        x2s, qxs = _mix_out_sample(x1s, oas.reshape(bs, d_attn), ys_t.reshape(bs, d_inner), woa, wos,
                                   row(g_cross[i]), wcq, scale=hd_x ** -0.5)
        ocs = _cross_sample(qxs.reshape(bs, n_heads_x, hd_x), cache_mem_k[i], cache_mem_v[i],
                            bb=2 if bs % 2 == 0 else 1)
        x3s = _cross_out_sample(x2s, ocs.reshape(bs, d), wco)
        ysm = _ffn(x3s, row(g_ffn2[i]), w2g, w2u, w2d, gfin, final_norm=last, tm=tm_s)
        outs[6].append(ks.reshape(bs, 1, n_heads_a, hd_a))
        outs[7].append(vs.reshape(bs, 1, n_heads_a, hd_a))
        outs[8].append(nconv.reshape(bs, CONV_WIDTH - 1, d_xbc))
        outs[9].append(nstate.reshape(bs, n_heads_b, hd_b, d_state))

    return (yp.reshape(b, s, d), ysm.reshape(bs, 1, d)) + tuple(jnp.stack(o) for o in outs)
```

```python
import functools
import math

import jax
import jax.numpy as jnp
from jax import lax
from jax.experimental import pallas as pl
from jax.experimental.pallas import tpu as pltpu

F32 = jnp.float32
BF16 = jnp.bfloat16

EPS = 1e-6
NEG_INF = -1e30
DILATIONS = (1, 4, 16)
N_BACK = 128
UNROLL = 8
MAX_DISTANCE = 2048
SSD_CHUNK = 128
CONV_WIDTH = 4

LANES = 128
SUBLANES = 8
VMEM_LIMIT_BYTES = 56 * 1024 * 1024

_HI = lax.Precision.HIGHEST


def _cparams(*sem):
    return pltpu.CompilerParams(dimension_semantics=sem, vmem_limit_bytes=VMEM_LIMIT_BYTES)


def _const_spec(shape):
    nd = len(shape)
    return pl.BlockSpec(shape, lambda *_: (0,) * nd, pipeline_mode=pl.Buffered(1))


def _rms(x, g):
    return x * lax.rsqrt(jnp.mean(x * x, axis=-1, keepdims=True) + EPS) * g


def _silu(x):
    h = 0.5 * x
    return h + h * jnp.tanh(h)


def _softplus(x):
    return jnp.maximum(x, 0.0) + jnp.log1p(jnp.exp(-jnp.abs(x)))


def _dot(a, b):
    return jnp.dot(a, b, preferred_element_type=F32)


def _dot_nt(a, b):
    return lax.dot_general(a, b, (((1,), (1,)), ((), ())), preferred_element_type=F32)


def _dot_tn(a, b, precision=None):
    return lax.dot_general(a, b, (((0,), (0,)), ((), ())), preferred_element_type=F32,
                           precision=precision)


class Guest:
    def __init__(self, body, operands, in_specs, out_shape, out_specs):
        self.body, self.operands, self.in_specs = body, list(operands), list(in_specs)
        self.out_shape, self.out_specs = list(out_shape), list(out_specs)

    def __add__(self, other):
        n_in, m_in, n_out = len(self.operands), len(other.operands), len(self.out_shape)

        def body(*refs):
            ins, outs = refs[:n_in + m_in], refs[n_in + m_in:]
            self.body(*ins[:n_in], *outs[:n_out])
            other.body(*ins[n_in:], *outs[n_out:])

        return Guest(body, self.operands + other.operands, self.in_specs + other.in_specs,
                     self.out_shape + other.out_shape, self.out_specs + other.out_specs)


def _ffn_body(x_ref, g_ref, wg_ref, wu_ref, wd_ref, gf_ref, *rest, final_norm, guest):
    n_in = len(guest.operands) if guest else 0
    guest_in, o_ref, guest_out = rest[:n_in], rest[n_in], rest[n_in + 1:]
    x = x_ref[...]
    xn = _rms(x, g_ref[...]).astype(BF16)
    gate = _dot(xn, wg_ref[...])
    up = _dot(xn, wu_ref[...])
    h = (_silu(gate) * up).astype(BF16)
    y = x + 0.5 * _dot(h, wd_ref[...])
    if final_norm:
        y = _rms(y, gf_ref[...])
    o_ref[...] = y
    if guest:
        guest.body(*guest_in, *guest_out)


def _ffn(x, g, wg, wu, wd, g_final, *, final_norm, tm, guest=None):
    n, d = x.shape
    ff = wg.shape[1]
    row = pl.BlockSpec((tm, d), lambda i: (i, 0))
    in_specs = [row, _const_spec((1, d)), _const_spec((d, ff)), _const_spec((d, ff)),
                _const_spec((ff, d)), _const_spec((1, d))]
    out = pl.pallas_call(
        functools.partial(_ffn_body, final_norm=final_norm, guest=guest),
        grid=(n // tm,),
        in_specs=in_specs + (guest.in_specs if guest else []),
        out_specs=[row] + (guest.out_specs if guest else []),
        out_shape=[jax.ShapeDtypeStruct((n, d), F32)] + (guest.out_shape if guest else []),
        compiler_params=_cparams("parallel"),
        name="ffn",
    )(x, g, wg, wu, wd, g_final, *(guest.operands if guest else []))
    return out if guest else out[0]


def _inproj_body(x_ref, g_ref, w_ref, q_ref, k_ref, v_ref, z_ref, xbc_ref, dt_ref, *, cuts, q_scale):
    xn = _rms(x_ref[...], g_ref[...]).astype(BF16)
    proj = _dot(xn, w_ref[...])
    c0, c1, c2, c3, c4 = cuts
    q_ref[...] = proj[:, :c0] * q_scale
    k_ref[...] = proj[:, c0:c1]
    v_ref[...] = proj[:, c1:c2]
    z_ref[...] = proj[:, c2:c3]
    xbc_ref[...] = proj[:, c3:c4]
    dt_ref[...] = proj[:, c4:]


def _inproj(x, g, w_pad, *, cuts, q_scale, tm):
    n, d = x.shape
    wp = w_pad.shape[1]
    c0, c1, c2, c3, c4 = cuts
    widths = (c0, c1 - c0, c2 - c1, c3 - c2, c4 - c3, wp - c4)
    row = lambda w: pl.BlockSpec((tm, w), lambda i: (i, 0))
    return pl.pallas_call(
        functools.partial(_inproj_body, cuts=cuts, q_scale=q_scale),
        grid=(n // tm,),
        in_specs=[row(d), _const_spec((1, d)), _const_spec((d, wp))],
        out_specs=[row(w) for w in widths],
        out_shape=[jax.ShapeDtypeStruct((n, w), F32) for w in widths],
        compiler_params=_cparams("parallel"),
        name="inproj",
    )(x, g, w_pad)


def _inproj_prompt_body(x_ref, g_ref, w_ref, cw_ref, cb_ref, dtb_ref,
                        q_ref, k_ref, v_ref, gate_ref, xbc_ref, dt_ref, kt_ref, vt_ref, tail_ref,
                        xpad_ref, *, cuts, q_scale, per_seq):
    tm = x_ref.shape[0]
    n_slabs = xpad_ref.shape[0]
    taps = cw_ref.shape[0]
    xn = _rms(x_ref[...], g_ref[...]).astype(BF16)
    c0, c1, c2, c3, c4 = cuts

    @pl.when(pl.program_id(0) % per_seq == 0)
    def _():
        xpad_ref[:, 0:SUBLANES, :] = jnp.zeros((n_slabs, SUBLANES, LANES), F32)

    raw = _dot(xn, w_ref[:, c3:c4])
    gate_ref[...] = _silu(_dot(xn, w_ref[:, c2:c3]))
    qkv = _dot(xn, w_ref[:, :c2])
    q_ref[...] = qkv[:, :c0] * q_scale
    k_ref[...] = qkv[:, c0:c1]
    v_ref[...] = qkv[:, c1:c2]
    kt_ref[...] = qkv[:, c0:c1].T
    vt_ref[...] = qkv[:, c1:c2].T
    dt_ref[...] = _softplus(_dot(xn, w_ref[:, c4:]) + dtb_ref[...])
    for j in range(n_slabs):
        xpad_ref[j, pl.ds(SUBLANES, tm), :] = raw[:, j * LANES:(j + 1) * LANES]
    cw = cw_ref[...]
    for j in range(n_slabs):
        sl = slice(j * LANES, (j + 1) * LANES)
        acc = cb_ref[:, sl]
        for k in range(taps):
            acc = acc + xpad_ref[j, pl.ds(SUBLANES - (taps - 1) + k, tm, stride=1), :] * cw[k:k + 1, sl]
        xbc_ref[:, sl] = _silu(acc)
    tail_ref[...] = raw[tm - SUBLANES:, :]
    xpad_ref[:, 0:SUBLANES, :] = xpad_ref[:, pl.ds(tm, SUBLANES), :]


def _inproj_prompt(x, g, w_pad, cw, cb, dtb, *, cuts, q_scale, tm, seq):
    n, d = x.shape
    wp = w_pad.shape[1]
    c0, c1, c2, c3, c4 = cuts
    widths = (c0, c1 - c0, c2 - c1, c3 - c2, c4 - c3, wp - c4)
    d_xbc = c4 - c3
    per_seq = seq // tm
    row = lambda w: pl.BlockSpec((tm, w), lambda i: (i, 0))
    feat = lambda w: pl.BlockSpec((None, w, tm), lambda i: (i // per_seq, 0, i % per_seq))
    out_specs = [row(w) for w in widths] + [feat(widths[1]), feat(widths[2]),
                                            pl.BlockSpec((None, SUBLANES, d_xbc), lambda i: (i // per_seq, 0, 0))]
    out_shape = [jax.ShapeDtypeStruct((n, w), F32) for w in widths] + [
        jax.ShapeDtypeStruct((n // seq, widths[1], seq), F32),
        jax.ShapeDtypeStruct((n // seq, widths[2], seq), F32),
        jax.ShapeDtypeStruct((n // seq, SUBLANES, d_xbc), F32)]
    return pl.pallas_call(
        functools.partial(_inproj_prompt_body, cuts=cuts, q_scale=q_scale, per_seq=per_seq),
        grid=(n // tm,),
        in_specs=[row(d), _const_spec((1, d)), _const_spec((d, wp)),
                  _const_spec(cw.shape), _const_spec(cb.shape), _const_spec(dtb.shape)],
        out_specs=out_specs,
        out_shape=out_shape,
        scratch_shapes=[pltpu.VMEM((d_xbc // LANES, SUBLANES + tm, LANES), F32)],
        compiler_params=_cparams("arbitrary"),
        name="inproj_prompt",
    )(x, g, w_pad, cw, cb, dtb)


def _t5_bucket(dist, num_buckets):
    max_exact = num_buckets // 2
    d = jnp.maximum(dist, 0)
    df = jnp.maximum(d, 1).astype(F32)
    large = max_exact + (jnp.log(df / max_exact) / math.log(MAX_DISTANCE / max_exact)
                         * (num_buckets - max_exact)).astype(jnp.int32)
    large = jnp.minimum(large, num_buckets - 1)
    return jnp.where(d < max_exact, d, large)


def _bias_body(rb_ref, idxp_ref, idxs_ref, cnt_ref, bp_ref, bs_ref, *, n_buckets, n_heads):
    n_pat = idxp_ref.shape[0]
    blk = idxp_ref.shape[1]
    qi = lax.broadcasted_iota(jnp.int32, (blk, 2 * blk), 0)
    kj = lax.broadcasted_iota(jnp.int32, (blk, 2 * blk), 1)
    delta = qi + blk - kj
    band = (delta >= 0) & (delta <= N_BACK)

    def lookup(idx, h):
        acc = jnp.zeros(idx.shape, F32)
        for c in range(n_buckets):
            acc = jnp.where(idx == c, rb_ref[c, h], acc)
        return acc

    for h in range(n_heads):
        for p in range(n_pat):
            bp_ref[p, h] = jnp.where(band, lookup(idxp_ref[p], h), NEG_INF)
        bs_ref[h:h + 1, :] = jnp.where(cnt_ref[...] > 0.0, lookup(idxs_ref[...], h), NEG_INF)


def _bias_tables(rel_bias, w_buf):
    n_buckets, n_heads = rel_bias.shape
    blk = N_BACK
    n_pat = len(DILATIONS)
    qi = jnp.arange(blk)[:, None]
    kj = jnp.arange(2 * blk)[None, :]
    delta = qi + blk - kj
    idx_p = jnp.stack([_t5_bucket(delta * d, n_buckets) for d in DILATIONS]).astype(jnp.int32)
    dist = jnp.concatenate([w_buf - jnp.arange(w_buf), jnp.zeros((LANES,), jnp.int32)])
    idx_s = _t5_bucket(dist, n_buckets).astype(jnp.int32)[None, :]
    count = sum(((dist % d == 0) & (dist <= d * N_BACK)).astype(F32) for d in DILATIONS)[None, :]
    vm = pl.BlockSpec(memory_space=pltpu.VMEM)
    bias_p, bias_s = pl.pallas_call(
        functools.partial(_bias_body, n_buckets=n_buckets, n_heads=n_heads),
        in_specs=[pl.BlockSpec(memory_space=pltpu.SMEM), vm, vm, vm],
        out_specs=[vm, vm],
        out_shape=[jax.ShapeDtypeStruct((n_pat, n_heads, blk, 2 * blk), F32),
                   jax.ShapeDtypeStruct((n_heads, w_buf + LANES), F32)],
        name="bias_tables",
    )(rel_bias, idx_p, idx_s, count)
    return bias_p, bias_s, count


def _attn_prompt_body(q_ref, k_ref, v_ref, bias_ref, o_ref, acc2, l2, m2, acc3, l3, m3, *, hd):
    seq = q_ref.shape[0]
    blk = N_BACK
    lane = lax.broadcasted_iota(jnp.int32, (1, LANES), 1)
    head_mask = (lane < hd, lane >= hd)

    def block(qb, kb, vb, bias2):
        nq = qb.shape[0]
        q2 = jnp.concatenate([jnp.where(head_mask[h], qb, 0.0).astype(BF16) for h in range(2)], axis=0)
        s = _dot_nt(q2, kb.astype(BF16)) + bias2
        m = jnp.max(s, axis=-1, keepdims=True)
        p = jnp.exp(s - m).astype(BF16)
        v2 = jnp.concatenate(
            [jnp.concatenate([jnp.where(head_mask[h], vb, 0.0).astype(BF16),
                              jnp.broadcast_to(jnp.where(head_mask[h], 1.0, 0.0).astype(BF16), vb.shape)], axis=1)
             for h in range(2)], axis=0)
        out = _dot(jnp.concatenate([p[:nq], p[nq:]], axis=1), v2)
        m_full = jnp.where(head_mask[0], m[:nq], m[nq:])
        return out[:, :LANES], out[:, LANES:], m_full

    def cur_bias(p):
        return bias_ref[p, :, :, blk:].reshape(2 * blk, blk)

    def full_bias(p):
        return bias_ref[p].reshape(2 * blk, 2 * blk)

    d3 = DILATIONS[2]

    def pat3(g, carry):
        for u in range(UNROLL):
            rows = pl.ds(g * UNROLL + u, blk, stride=d3)
            a, l, m = block(q_ref[rows, :], k_ref[rows, :], v_ref[rows, :], cur_bias(2))
            acc3[rows, :] = a
            l3[rows, :] = l
            m3[rows, :] = m
        return carry

    lax.fori_loop(0, d3 // UNROLL, pat3, 0)

    d2 = DILATIONS[1]
    nb2 = seq // (d2 * blk)

    classes_per_body = max(1, UNROLL // nb2)

    def pat2(g, carry):
        for u in range(classes_per_body):
            r = g * classes_per_body + u
            for i in range(nb2):
                rows = pl.ds(r + d2 * blk * i, blk, stride=d2)
                if i == 0:
                    krows, bias = rows, cur_bias(1)
                else:
                    krows, bias = pl.ds(r + d2 * blk * (i - 1), 2 * blk, stride=d2), full_bias(1)
                a, l, m = block(q_ref[rows, :], k_ref[krows, :], v_ref[krows, :], bias)
                acc2[rows, :] = a
                l2[rows, :] = l
                m2[rows, :] = m
        return carry

    lax.fori_loop(0, d2 // classes_per_body, pat2, 0)

    def finish(rows, a1, l1, m1):
        a2_, l2_, m2_ = acc2[rows, :], l2[rows, :], m2[rows, :]
        a3_, l3_, m3_ = acc3[rows, :], l3[rows, :], m3[rows, :]
        mx = jnp.maximum(jnp.maximum(m1, m2_), m3_)
        w1, w2, w3 = jnp.exp(m1 - mx), jnp.exp(m2_ - mx), jnp.exp(m3_ - mx)
        num = w1 * a1 + w2 * a2_ + w3 * a3_
        den = w1 * l1 + w2 * l2_ + w3 * l3_
        o_ref[rows, :] = (num / den).astype(o_ref.dtype)

    def pat1_block(start):
        rows = pl.ds(start, blk)
        kstart = start - blk if isinstance(start, int) else pl.multiple_of(start - blk, blk)
        krows = pl.ds(kstart, 2 * blk)
        finish(rows, *block(q_ref[rows, :], k_ref[krows, :], v_ref[krows, :], full_bias(0)))

    rows0 = pl.ds(0, blk)
    finish(rows0, *block(q_ref[rows0, :], k_ref[rows0, :], v_ref[rows0, :], cur_bias(0)))
    for u in range(1, UNROLL):
        pat1_block(u * blk)

    def pat1(g, carry):
        for u in range(UNROLL):
            pat1_block(pl.multiple_of((g * UNROLL + u) * blk, blk))
        return carry

    lax.fori_loop(1, seq // (blk * UNROLL), pat1, 0)


def _attn_prompt(q, k, v, bias_p, *, hd):
    b, s, da = q.shape
    n_pairs = da // LANES
    n_pat = bias_p.shape[0]
    blk = N_BACK
    tok = pl.BlockSpec((None, s, LANES), lambda i, j: (i, 0, j))
    return pl.pallas_call(
        functools.partial(_attn_prompt_body, hd=hd),
        grid=(b, n_pairs),
        in_specs=[tok, tok, tok,
                  pl.BlockSpec((n_pat, 2, blk, 2 * blk), lambda i, j: (0, j, 0, 0))],
        out_specs=tok,
        out_shape=jax.ShapeDtypeStruct((b, s, da), BF16),
        scratch_shapes=[pltpu.VMEM((s, LANES), F32) for _ in range(6)],
        compiler_params=_cparams("parallel", "arbitrary"),
        name="attn_prompt",
    )(q, k, v, bias_p)


def _ssd_prompt_body(gate_ref, xbc_ref, dt_ref, alog_ref, dskip_ref, gssm_ref,
                     y_ref, hout_ref, ht_ref, *, n_heads, hd, n_groups, d_state):
    step = pl.program_id(1)
    ts = gate_ref.shape[0]
    q = SSD_CHUNK
    d_inner = n_heads * hd
    gw = d_inner // n_groups
    pairs_per_group = gw // LANES

    @pl.when(step == 0)
    def _():
        ht_ref[...] = jnp.zeros_like(ht_ref)

    lane = lax.broadcasted_iota(jnp.int32, (1, LANES), 1)
    head_mask = (lane < hd, lane >= hd)
    row_q = lax.broadcasted_iota(jnp.int32, (q, q), 0)
    col_q = lax.broadcasted_iota(jnp.int32, (q, q), 1)
    causal = row_q >= col_q
    tri = jnp.where(causal, 1.0, 0.0).astype(BF16)

    neg_a_log2e = -jnp.exp(alog_ref[...]) * math.log2(math.e)

    def chunk(c, carry):
        t0 = pl.multiple_of(c * q, q)
        rows = pl.ds(t0, q)
        xs = xbc_ref[rows, :d_inner]
        bm = xbc_ref[rows, d_inner:d_inner + n_groups * d_state]
        cm = xbc_ref[rows, d_inner + n_groups * d_state:]

        dt = dt_ref[rows, :]
        la = dt * neg_a_log2e
        la_hi = la.astype(BF16)
        r1 = la - la_hi.astype(F32)
        la_mid = r1.astype(BF16)
        la_lo = (r1 - la_mid.astype(F32)).astype(BF16)
        a_cs = _dot(tri, la_hi) + _dot(tri, la_mid) + _dot(tri, la_lo)
        total = a_cs[q - 1:q, :]
        w_s = dt * jnp.exp2(total - a_cs)
        chunk_decay = jnp.exp2(total)
        a_cs_t, dt_t, w_t = a_cs.T, dt.T, w_s.T

        y_tiles = []
        for g in range(n_groups):
            b_g = bm[:, g * d_state:(g + 1) * d_state]
            c_g = cm[:, g * d_state:(g + 1) * d_state]
            cb = _dot_nt(c_g.astype(BF16), b_g.astype(BF16))
            b_t = b_g.T
            for jl in range(pairs_per_group):
                j = g * pairs_per_group + jl
                pair = slice(jl * LANES, (jl + 1) * LANES)
                xs_pair = xs[:, j * LANES:(j + 1) * LANES]
                ht_pair = ht_ref[g, :, pair]
                y_pair = None
                st_pair = None
                cd_pair = None
                for h in range(2):
                    e = 2 * j + h
                    col = jnp.broadcast_to(a_cs[:, e:e + 1], (q, q))
                    lmat = jnp.exp2(jnp.where(causal, col - a_cs_t[e:e + 1, :], NEG_INF))
                    col_n = col if d_state == q else jnp.broadcast_to(a_cs[:, e:e + 1], (q, d_state))
                    lhs = jnp.concatenate([(cb * lmat * dt_t[e:e + 1, :]).astype(BF16),
                                           (c_g * jnp.exp2(col_n)).astype(BF16)], axis=1)
                    xs_h = jnp.where(head_mask[h], xs_pair, 0.0).astype(BF16)
                    rhs = jnp.concatenate([xs_h, jnp.where(head_mask[h], ht_pair, 0.0).astype(BF16)], axis=0)
                    yh = _dot(lhs, rhs)
                    sth = _dot((b_t * w_t[e:e + 1, :]).astype(BF16), xs_h)
                    cdh = chunk_decay[:, e:e + 1]
                    y_pair = yh if y_pair is None else y_pair + yh
                    st_pair = sth if st_pair is None else st_pair + sth
                    cd_pair = cdh if cd_pair is None else jnp.where(head_mask[0], cd_pair, cdh)
                y_tiles.append(y_pair + dskip_ref[:, j * LANES:(j + 1) * LANES] * xs_pair)
                ht_ref[g, :, pair] = ht_pair * cd_pair + st_pair

        y = jnp.concatenate(y_tiles, axis=1)
        yg = y * gate_ref[rows, :]
        parts = []
        for g in range(n_groups):
            part = yg[:, g * gw:(g + 1) * gw]
            parts.append(part * lax.rsqrt(jnp.mean(part * part, axis=-1, keepdims=True) + EPS))
        y_ref[rows, :] = (jnp.concatenate(parts, axis=1) * gssm_ref[...]).astype(y_ref.dtype)
        return carry

    lax.fori_loop(0, ts // q, chunk, 0)

    @pl.when(step == pl.num_programs(1) - 1)
    def _():
        for g in range(n_groups):
            hout_ref[g * gw:(g + 1) * gw, :] = ht_ref[g].T


def _ssd_prompt(gate, xbc, dt, alog, dskip, gssm, *, n_heads, hd, n_groups, d_state, ts):
    b, s, d_inner = gate.shape
    d_xbc = xbc.shape[-1]
    gw = d_inner // n_groups
    tok = lambda w: pl.BlockSpec((None, ts, w), lambda i, j: (i, j, 0))
    return pl.pallas_call(
        functools.partial(_ssd_prompt_body, n_heads=n_heads, hd=hd, n_groups=n_groups, d_state=d_state),
        grid=(b, s // ts),
        in_specs=[tok(d_inner), tok(d_xbc), tok(LANES),
                  _const_spec(alog.shape), _const_spec(dskip.shape), _const_spec(gssm.shape)],
        out_specs=[tok(d_inner), pl.BlockSpec((None, d_inner, d_state), lambda i, j: (i, 0, 0))],
        out_shape=[jax.ShapeDtypeStruct((b, s, d_inner), BF16),
                   jax.ShapeDtypeStruct((b, d_inner, d_state), F32)],
        scratch_shapes=[pltpu.VMEM((n_groups, d_state, gw), F32)],
        compiler_params=_cparams("parallel", "arbitrary"),
        name="ssd_prompt",
    )(gate, xbc, dt, alog, dskip, gssm)


def _memkv_body(m_ref, g_ref, wk_ref, wv_ref, k_ref, v_ref):
    mn = _rms(m_ref[...], g_ref[...]).astype(BF16)
    k_ref[...] = _dot(mn, wk_ref[...])
    v_ref[...] = _dot(mn, wv_ref[...])


def _memkv(mem, g, wk, wv, *, tm):
    n, d = mem.shape
    dk = wk.shape[1]
    row = lambda w: pl.BlockSpec((tm, w), lambda i: (i, 0))
    return pl.pallas_call(
        _memkv_body,
        grid=(n // tm,),
        in_specs=[row(d), _const_spec((1, d)), _const_spec(wk.shape), _const_spec(wv.shape)],
        out_specs=[row(dk), row(dk)],
        out_shape=[jax.ShapeDtypeStruct((n, dk), F32)] * 2,
        compiler_params=_cparams("parallel"),
        name="memkv",
    )(mem, g, wk, wv)


def _post_prompt_body(x_ref, oa_ref, ys_ref, mk_ref, mv_ref, woa_ref, wos_ref, gc_ref, wcq_ref, wco_ref,
                      o_ref, *, n_heads_x):
    x2 = x_ref[...] + _dot(oa_ref[...], woa_ref[...]) + _dot(ys_ref[...], wos_ref[...])
    dx = x2.shape[1] // n_heads_x
    hq = _rms(x2, gc_ref[...]).astype(BF16)
    qx = (_dot(hq, wcq_ref[...]) * (dx ** -0.5)).astype(BF16)
    mk = mk_ref[...].astype(BF16)
    mv = mv_ref[...].astype(BF16)
    outs = []
    for h in range(n_heads_x):
        sl = slice(h * dx, (h + 1) * dx)
        s = _dot_nt(qx[:, sl], mk[:, sl])
        p = jnp.exp(s - jnp.max(s, axis=-1, keepdims=True))
        l = jnp.sum(p, axis=-1, keepdims=True)
        outs.append(_dot(p.astype(BF16), mv[:, sl]) / l)
    oc = jnp.concatenate(outs, axis=1).astype(BF16)
    o_ref[...] = x2 + _dot(oc, wco_ref[...])


def _post_prompt(x1, oa, ys, mk, mv, woa, wos, gc, wcq, wco, *, n_heads_x, tm):
    b, s, d = x1.shape
    n_mem = mk.shape[1]
    tok = lambda w: pl.BlockSpec((None, tm, w), lambda i, j: (i, j, 0))
    mem = pl.BlockSpec((None, n_mem, d), lambda i, j: (i, 0, 0))
    return pl.pallas_call(
        functools.partial(_post_prompt_body, n_heads_x=n_heads_x),
        grid=(b, s // tm),
        in_specs=[tok(d), tok(oa.shape[-1]), tok(ys.shape[-1]), mem, mem,
                  _const_spec(woa.shape), _const_spec(wos.shape), _const_spec(gc.shape),
                  _const_spec(wcq.shape), _const_spec(wco.shape)],
        out_specs=tok(d),
        out_shape=jax.ShapeDtypeStruct((b, s, d), F32),
        compiler_params=_cparams("parallel", "arbitrary"),
        name="post_prompt",
    )(x1, oa, ys, mk, mv, woa, wos, gc, wcq, wco)


def _attn_sample_body(q_ref, kn_ref, vn_ref, kt_ref, vt_ref, bias_ref, cnt_ref, o_ref, *, hd):
    da, w = kt_ref.shape
    n_heads = da // hd
    row = lax.broadcasted_iota(jnp.int32, (n_heads, da), 0)
    col = lax.broadcasted_iota(jnp.int32, (n_heads, da), 1)
    hmask = (col >= row * hd) & (col < (row + 1) * hd)
    qbd16 = jnp.where(hmask, q_ref[...], 0.0).astype(BF16)
    s = _dot(qbd16, kt_ref[...].astype(BF16)) + bias_ref[:, :w]
    kn = kn_ref[...].astype(BF16).astype(F32)
    s_new = jnp.sum(qbd16.astype(F32) * kn, axis=-1, keepdims=True) + bias_ref[:, w:w + 1]
    m = jnp.maximum(jnp.max(s, axis=-1, keepdims=True), s_new)
    e = jnp.exp(s - m) * cnt_ref[:, :w]
    e_new = jnp.exp(s_new - m) * cnt_ref[:, w:w + 1]
    den = jnp.sum(e, axis=-1, keepdims=True) + e_new
    acc = _dot_nt(e.astype(BF16), vt_ref[...].astype(BF16)) + e_new * vn_ref[...]
    o_ref[...] = jnp.sum(jnp.where(hmask, acc / den, 0.0), axis=0, keepdims=True)


def _attn_sample_guest(q, kn, vn, kct, vct, bias_s, count, *, hd):
    bs, _, da = q.shape
    w = kct.shape[-1]
    row = pl.BlockSpec((None, 1, da), lambda i: (i, 0, 0))
    cache = pl.BlockSpec((None, da, w), lambda i: (i, 0, 0))
    return Guest(functools.partial(_attn_sample_body, hd=hd),
                 (q, kn, vn, kct, vct, bias_s, count),
                 [row, row, row, cache, cache, _const_spec(bias_s.shape), _const_spec(count.shape)],
                 [jax.ShapeDtypeStruct((bs, 1, da), F32)], [row])


def _ssd_sample_body(z_ref, xbc_ref, dt_ref, cc_ref, st_ref, cw_ref, cb_ref, dtb_ref, alog_ref, dskip_ref,
                     gssm_ref, y_ref, nconv_ref, nst_ref, *, n_groups):
    n_pairs = z_ref.shape[0]
    pairs_per_group = n_pairs // n_groups
    gw = pairs_per_group * LANES
    row8 = lax.broadcasted_iota(jnp.int32, (n_pairs, 1), 0)
    neg_a = -jnp.exp(alog_ref[...])
    pad = jnp.zeros((LANES - 2 * n_pairs, LANES), F32)

    cc = cc_ref[...]
    xr = xbc_ref[...]
    pre = xr * cw_ref[CONV_WIDTH - 1] + cb_ref[...]
    for k in range(CONV_WIDTH - 1):
        pre = pre + cc[k] * cw_ref[k]
        nconv_ref[k] = cc[k + 1] if k + 1 < CONV_WIDTH - 1 else xr
    act = _silu(pre)
    xs = act[:n_pairs]
    bm = act[n_pairs:n_pairs + n_groups]
    cm = act[n_pairs + n_groups:]
    dt = _softplus(dt_ref[...] + dtb_ref[...])
    d_a = jnp.exp(dt * neg_a)
    cols = jnp.concatenate([xs * dt, d_a, pad], axis=0).T
    y = jnp.zeros((n_pairs, LANES), F32)
    for j in range(n_pairs):
        g = j // pairs_per_group
        rows = pl.ds(j * LANES, LANES)
        h_new = cols[:, n_pairs + j:n_pairs + j + 1] * st_ref[rows, :] + cols[:, j:j + 1] * bm[g:g + 1, :]
        nst_ref[rows, :] = h_new
        c_sel = jnp.where(row8 == j, cm[g:g + 1, :], 0.0).astype(BF16)
        y = y + _dot_nt(c_sel, h_new.astype(BF16))
    y = y + dskip_ref[...] * xs
    yg = y * _silu(z_ref[...])
    ss = jnp.sum(yg * yg, axis=-1, keepdims=True)
    mean = jnp.zeros_like(ss)
    for g in range(n_groups):
        in_g = (row8 >= g * pairs_per_group) & (row8 < (g + 1) * pairs_per_group)
        tot = jnp.sum(jnp.where(in_g, ss, 0.0), axis=0, keepdims=True)
        mean = jnp.where(in_g, tot / gw, mean)
    y_ref[...] = yg * lax.rsqrt(mean + EPS) * gssm_ref[...]


def _ssd_sample_guest(z, xbc, dt, cc, st, cw, cb, dtb, alog, dskip, gssm, *, n_groups):
    seq_block = lambda a: pl.BlockSpec((None,) + a.shape[1:], lambda i: (i,) + (0,) * (a.ndim - 1))
    consts = (cw, cb, dtb, alog, dskip, gssm)
    return Guest(functools.partial(_ssd_sample_body, n_groups=n_groups),
                 (z, xbc, dt, cc, st) + consts,
                 [seq_block(a) for a in (z, xbc, dt, cc, st)] + [_const_spec(a.shape) for a in consts],
                 [jax.ShapeDtypeStruct(a.shape, F32) for a in (z, cc, st)],
                 [seq_block(a) for a in (z, cc, st)])


def _mix_out_sample_body(x_ref, oa_ref, ys_ref, woa_ref, wos_ref, gc_ref, wcq_ref, x2_ref, qx_ref, *, scale):
    x2 = x_ref[...] + _dot(oa_ref[...].astype(BF16), woa_ref[...]) + _dot(ys_ref[...].astype(BF16), wos_ref[...])
    x2_ref[...] = x2
    qx_ref[...] = _dot(_rms(x2, gc_ref[...]).astype(BF16), wcq_ref[...]) * scale


def _mix_out_sample(x1, oa, ys, woa, wos, gc, wcq, *, scale):
    vm = pl.BlockSpec(memory_space=pltpu.VMEM)
    return pl.pallas_call(
        functools.partial(_mix_out_sample_body, scale=scale),
        in_specs=[vm] * 7,
        out_specs=[vm, vm],
        out_shape=[jax.ShapeDtypeStruct(x1.shape, F32), jax.ShapeDtypeStruct((x1.shape[0], wcq.shape[1]), F32)],
        compiler_params=pltpu.CompilerParams(vmem_limit_bytes=VMEM_LIMIT_BYTES),
        name="mix_out_sample",
    )(x1, oa, ys, woa, wos, gc, wcq)


def _cross_sample_body(q_ref, mk_ref, mv_ref, o_ref):
    s = jnp.sum(mk_ref[...] * q_ref[...][None], axis=-1, keepdims=True)
    p = jnp.exp(s - jnp.max(s, axis=0, keepdims=True))
    l = jnp.sum(p, axis=0)
    o_ref[...] = jnp.sum(p * mv_ref[...], axis=0) / l


def _cross_sample_guest(qx, mk, mv):
    bs, n_mem, nh, dx = mk.shape
    row = pl.BlockSpec((None, nh, dx), lambda i: (i, 0, 0))
    mem = pl.BlockSpec((None, n_mem, nh, dx), lambda i: (i, 0, 0, 0))
    return Guest(_cross_sample_body, (qx, mk, mv), [row, mem, mem],
                 [jax.ShapeDtypeStruct((bs, nh, dx), F32)], [row])


def _cross_out_sample_body(x_ref, oc_ref, wco_ref, o_ref):
    o_ref[...] = x_ref[...] + _dot(oc_ref[...].astype(BF16), wco_ref[...])


def _cross_out_sample(x2, oc, wco):
    vm = pl.BlockSpec(memory_space=pltpu.VMEM)
    return pl.pallas_call(
        _cross_out_sample_body,
        in_specs=[vm] * 3,
        out_specs=vm,
        out_shape=jax.ShapeDtypeStruct(x2.shape, F32),
        name="cross_out_sample",
    )(x2, oc, wco)


def _tile(n, target):
    t = min(n, target)
    while n % t or (t % SUBLANES and t != n):
        t -= 1
    return t


def kernel(x_prompt, x_sample, cache_win_k, cache_win_v, cache_conv, state_ssm, cache_mem_k, cache_mem_v, mem_prompt, rel_bias, g_ffn1, w1_gate, w1_up, w1_down, g_mix, w_in, conv_w, conv_b, dt_bias, a_log, d_skip, g_ssm, w_out, g_mem, w_ck, w_cv, g_cross, w_cq, w_co, g_ffn2, w2_gate, w2_up, w2_down, g_final):
    depth = g_ffn1.shape[0]
    b, s, d = x_prompt.shape
    bs, dec_seq, _ = x_sample.shape
    assert dec_seq == 1, "sample path handles one new token per sequence"
    n_heads_a, hd_a = cache_win_k.shape[-2:]
    d_attn = n_heads_a * hd_a
    n_heads_b, hd_b, d_state = state_ssm.shape[-3:]
    d_inner = n_heads_b * hd_b
    d_xbc = conv_w.shape[-1]
    n_groups = (d_xbc - d_inner) // (2 * d_state)
    n_mem, n_heads_x, hd_x = cache_mem_k.shape[-3:]
    w_buf = cache_win_k.shape[2]
    assert hd_a * 2 == LANES and hd_b * 2 == LANES and d_state == LANES
    assert s % (DILATIONS[-1] * N_BACK) == 0 and s >= CONV_WIDTH - 1

    cuts = (d_attn, 2 * d_attn, 3 * d_attn, 3 * d_attn + d_inner, 3 * d_attn + d_inner + d_xbc)
    d_in = w_in.shape[-1]
    d_in_pad = cuts[-1] + LANES
    n_pairs = d_inner // LANES
    xbc_rows = d_xbc // LANES

    def row(a):
        return a.reshape(1, -1)

    def per_feature(a):
        return jnp.repeat(a, hd_b).reshape(1, d_inner)

    def head_lanes(a):
        return jnp.pad(a, (0, LANES - n_heads_b)).reshape(1, LANES)

    bias_p, bias_s, count_s = _bias_tables(rel_bias, w_buf)

    yp = x_prompt.reshape(b * s, d)
    ysm = x_sample.reshape(bs, d)
    tm_p = _tile(b * s, 512)
    tm_s = _tile(bs, 512)
    assert (b * s) // tm_p == bs, "guest kernels take one sample sequence per prompt grid step"
    gfin = row(g_final)
    outs = [[] for _ in range(10)]
    for i in range(depth):
        bf = lambda a: a[i].astype(BF16)
        w1g, w1u, w1d = bf(w1_gate), bf(w1_up), bf(w1_down)
        w2g, w2u, w2d = bf(w2_gate), bf(w2_up), bf(w2_down)
        w_in_p = jnp.pad(w_in[i], ((0, 0), (0, d_in_pad - d_in))).astype(BF16)
        woa, wos = w_out[i, :d_attn].astype(BF16), w_out[i, d_attn:].astype(BF16)
        wck, wcv, wcq, wco = bf(w_ck), bf(w_cv), bf(w_cq), bf(w_co)
        last = i == depth - 1

        x1s = _ffn(ysm, row(g_ffn1[i]), w1g, w1u, w1d, gfin, final_norm=False, tm=tm_s)
        qs, ks, vs, zs, xbcs, dts = _inproj(x1s, row(g_mix[i]), w_in_p, cuts=cuts, q_scale=hd_a ** -0.5, tm=tm_s)
        r3 = lambda a: a.reshape(bs, 1, a.shape[-1])
        to_t = lambda c: c.transpose(0, 2, 3, 1).reshape(bs, d_attn, w_buf)
        attn_s = _attn_sample_guest(r3(qs), r3(ks), r3(vs), to_t(cache_win_k[i]), to_t(cache_win_v[i]),
                                    bias_s, count_s, hd=hd_a)
        tiles = lambda a: a.reshape(a.shape[:-1] + (a.shape[-1] // LANES, LANES))
        dt_feat = jnp.repeat(dts[:, :n_heads_b], hd_b, axis=-1)
        ssd_s = _ssd_sample_guest(
            tiles(zs), tiles(xbcs), tiles(dt_feat), tiles(cache_conv[i]),
            state_ssm[i].reshape(bs, d_inner, d_state),
            tiles(conv_w[i]), tiles(conv_b[i]), tiles(jnp.repeat(dt_bias[i], hd_b)),
            tiles(jnp.repeat(a_log[i], hd_b)), tiles(jnp.repeat(d_skip[i], hd_b)), tiles(g_ssm[i]),
            n_groups=n_groups)

        x1, oas, ys_t, nconv, nstate = _ffn(yp, row(g_ffn1[i]), w1g, w1u, w1d, gfin, final_norm=False,
                                            tm=tm_p, guest=attn_s + ssd_s)
        q, k, v, gate, xbc, dtp, kt, vt, xbc_tail = _inproj_prompt(
            x1, row(g_mix[i]), w_in_p, conv_w[i], row(conv_b[i]), head_lanes(dt_bias[i]),
            cuts=cuts, q_scale=hd_a ** -0.5, tm=_tile(s, 512), seq=s)
        to_seq = lambda a: a.reshape(b, s, a.shape[-1])
        oa = _attn_prompt(to_seq(q), to_seq(k), to_seq(v), bias_p, hd=hd_a)
        y_ssm, h_last = _ssd_prompt(
            to_seq(gate), to_seq(xbc), to_seq(dtp), head_lanes(a_log[i]), per_feature(d_skip[i]), row(g_ssm[i]),
            n_heads=n_heads_b, hd=hd_b, n_groups=n_groups, d_state=d_state, ts=_tile(s, 512))
        mk, mv = _memkv(mem_prompt.reshape(b * n_mem, d), row(g_mem[i]), wck, wcv, tm=_tile(b * n_mem, 512))
        x3 = _post_prompt(to_seq(x1), oa, y_ssm, mk.reshape(b, n_mem, d), mv.reshape(b, n_mem, d),
                          woa, wos, row(g_cross[i]), wcq, wco, n_heads_x=n_heads_x, tm=_tile(s, 512))
        x2s, qxs = _mix_out_sample(x1s, oas.reshape(bs, d_attn), ys_t.reshape(bs, d_inner), woa, wos,
                                   row(g_cross[i]), wcq, scale=hd_x ** -0.5)
        cross_s = _cross_sample_guest(qxs.reshape(bs, n_heads_x, hd_x), cache_mem_k[i], cache_mem_v[i])
        yp, ocs = _ffn(x3.reshape(b * s, d), row(g_ffn2[i]), w2g, w2u, w2d, gfin, final_norm=last, tm=tm_p,
                       guest=cross_s)
        keep = min(MAX_DISTANCE, s)
        from_t = lambda a: a.reshape(b, n_heads_a, hd_a, s).transpose(0, 3, 1, 2)[:, s - keep:]
        outs[0].append(from_t(kt))
        outs[1].append(from_t(vt))
        outs[2].append(xbc_tail[:, SUBLANES - (CONV_WIDTH - 1):])
        outs[3].append(h_last.reshape(b, n_heads_b, hd_b, d_state))
        outs[4].append(mk.reshape(b, n_mem, n_heads_x, hd_x))
        outs[5].append(mv.reshape(b, n_mem, n_heads_x, hd_x))

        x3s = _cross_out_sample(x2s, ocs.reshape(bs, d), wco)
        ysm = _ffn(x3s, row(g_ffn2[i]), w2g, w2u, w2d, gfin, final_norm=last, tm=tm_s)
        outs[6].append(ks.reshape(bs, 1, n_heads_a, hd_a))
        outs[7].append(vs.reshape(bs, 1, n_heads_a, hd_a))
        outs[8].append(nconv.reshape(bs, CONV_WIDTH - 1, d_xbc))
        outs[9].append(nstate.reshape(bs, n_heads_b, hd_b, d_state))

    return (yp.reshape(b, s, d), ysm.reshape(bs, 1, d)) + tuple(jnp.stack(o) for o in outs)
```

```python
import functools
import math

import jax
import jax.numpy as jnp
from jax import lax
from jax.experimental import pallas as pl
from jax.experimental.pallas import tpu as pltpu

F32 = jnp.float32
BF16 = jnp.bfloat16

EPS = 1e-6
NEG_INF = -1e30
DILATIONS = (1, 4, 16)
N_BACK = 128
UNROLL = 8
ROW_SPLIT = 2
MAX_DISTANCE = 2048
SSD_CHUNK = 128
CONV_WIDTH = 4

LANES = 128
SUBLANES = 8
VMEM_LIMIT_BYTES = 56 * 1024 * 1024

_HI = lax.Precision.HIGHEST


def _cparams(*sem):
    return pltpu.CompilerParams(dimension_semantics=sem, vmem_limit_bytes=VMEM_LIMIT_BYTES)


def _const_spec(shape):
    nd = len(shape)
    return pl.BlockSpec(shape, lambda *_: (0,) * nd, pipeline_mode=pl.Buffered(1))


def _seq_spec(tail, seq_of):
    return pl.BlockSpec((None,) + tuple(tail), lambda *g: (seq_of(*g),) + (0,) * len(tail))


def _rms(x, g):
    return x * lax.rsqrt(jnp.mean(x * x, axis=-1, keepdims=True) + EPS) * g


def _silu(x):
    h = 0.5 * x
    return h + h * jnp.tanh(h)


def _softplus(x):
    return jnp.maximum(x, 0.0) + jnp.log1p(jnp.exp(-jnp.abs(x)))


def _dot(a, b):
    return jnp.dot(a, b, preferred_element_type=F32)


def _dot_nt(a, b):
    return lax.dot_general(a, b, (((1,), (1,)), ((), ())), preferred_element_type=F32)


def _dot_tn(a, b, precision=None):
    return lax.dot_general(a, b, (((0,), (0,)), ((), ())), preferred_element_type=F32,
                           precision=precision)


class Guest:
    def __init__(self, body, operands, in_specs, out_shape, out_specs):
        self.body, self.operands, self.in_specs = body, list(operands), list(in_specs)
        self.out_shape, self.out_specs = list(out_shape), list(out_specs)

    def __add__(self, other):
        n_in, m_in, n_out = len(self.operands), len(other.operands), len(self.out_shape)

        def body(*refs):
            ins, outs = refs[:n_in + m_in], refs[n_in + m_in:]
            self.body(*ins[:n_in], *outs[:n_out])
            other.body(*ins[n_in:], *outs[n_out:])

        return Guest(body, self.operands + other.operands, self.in_specs + other.in_specs,
                     self.out_shape + other.out_shape, self.out_specs + other.out_specs)


def _ffn_body(x_ref, g_ref, wg_ref, wu_ref, wd_ref, gf_ref, *rest, final_norm, guest):
    n_in = len(guest.operands) if guest else 0
    guest_in, o_ref, guest_out = rest[:n_in], rest[n_in], rest[n_in + 1:]
    x = x_ref[...]
    xn = _rms(x, g_ref[...]).astype(BF16)
    gate = _dot(xn, wg_ref[...])
    up = _dot(xn, wu_ref[...])
    h = (_silu(gate) * up).astype(BF16)
    y = x + 0.5 * _dot(h, wd_ref[...])
    if final_norm:
        y = _rms(y, gf_ref[...])
    o_ref[...] = y
    if guest:
        guest.body(*guest_in, *guest_out)


def _ffn(x, g, wg, wu, wd, g_final, *, final_norm, tm, guest=None):
    n, d = x.shape
    ff = wg.shape[1]
    row = pl.BlockSpec((tm, d), lambda i: (i, 0))
    in_specs = [row, _const_spec((1, d)), _const_spec((d, ff)), _const_spec((d, ff)),
                _const_spec((ff, d)), _const_spec((1, d))]
    out = pl.pallas_call(
        functools.partial(_ffn_body, final_norm=final_norm, guest=guest),
        grid=(n // tm,),
        in_specs=in_specs + (guest.in_specs if guest else []),
        out_specs=[row] + (guest.out_specs if guest else []),
        out_shape=[jax.ShapeDtypeStruct((n, d), F32)] + (guest.out_shape if guest else []),
        compiler_params=_cparams("parallel"),
        name="ffn",
    )(x, g, wg, wu, wd, g_final, *(guest.operands if guest else []))
    return out if guest else out[0]


def _inproj_body(x_ref, g_ref, w_ref, q_ref, k_ref, v_ref, z_ref, xbc_ref, dt_ref, *, cuts, q_scale):
    xn = _rms(x_ref[...], g_ref[...]).astype(BF16)
    proj = _dot(xn, w_ref[...])
    c0, c1, c2, c3, c4 = cuts
    q_ref[...] = proj[:, :c0] * q_scale
    k_ref[...] = proj[:, c0:c1]
    v_ref[...] = proj[:, c1:c2]
    z_ref[...] = proj[:, c2:c3]
    xbc_ref[...] = proj[:, c3:c4]
    dt_ref[...] = proj[:, c4:]


def _inproj(x, g, w_pad, *, cuts, q_scale, tm):
    n, d = x.shape
    wp = w_pad.shape[1]
    c0, c1, c2, c3, c4 = cuts
    widths = (c0, c1 - c0, c2 - c1, c3 - c2, c4 - c3, wp - c4)
    row = lambda w: pl.BlockSpec((tm, w), lambda i: (i, 0))
    return pl.pallas_call(
        functools.partial(_inproj_body, cuts=cuts, q_scale=q_scale),
        grid=(n // tm,),
        in_specs=[row(d), _const_spec((1, d)), _const_spec((d, wp))],
        out_specs=[row(w) for w in widths],
        out_shape=[jax.ShapeDtypeStruct((n, w), F32) for w in widths],
        compiler_params=_cparams("parallel"),
        name="inproj",
    )(x, g, w_pad)


def _inproj_prompt_body(x_ref, g_ref, w_ref, cw_ref, cb_ref, dtb_ref,
                        q_ref, k_ref, v_ref, gate_ref, xbc_ref, dt_ref, kt_ref, vt_ref, tail_ref,
                        xpad_ref, *, cuts, q_scale, per_seq):
    tm = x_ref.shape[0]
    n_slabs = xpad_ref.shape[0]
    taps = cw_ref.shape[0]
    c0, c1, c2, c3, c4 = cuts
    cw = cw_ref[...]

    @pl.when(pl.program_id(0) % per_seq == 0)
    def _():
        xpad_ref[:, 0:SUBLANES, :] = jnp.zeros((n_slabs, SUBLANES, LANES), F32)

    half = tm // ROW_SPLIT
    for r0 in range(0, tm, half):
        rows = pl.ds(r0, half)
        xn = _rms(x_ref[rows, :], g_ref[...]).astype(BF16)
        raw = _dot(xn, w_ref[:, c3:c4])
        gate_ref[rows, :] = _silu(_dot(xn, w_ref[:, c2:c3]))
        qkv = _dot(xn, w_ref[:, :c2])
        q_ref[rows, :] = qkv[:, :c0] * q_scale
        k_ref[rows, :] = qkv[:, c0:c1]
        v_ref[rows, :] = qkv[:, c1:c2]
        kt_ref[:, rows] = qkv[:, c0:c1].T
        vt_ref[:, rows] = qkv[:, c1:c2].T
        dt_ref[rows, :] = _softplus(_dot(xn, w_ref[:, c4:]) + dtb_ref[...])
        for j in range(n_slabs):
            xpad_ref[j, pl.ds(SUBLANES + r0, half), :] = raw[:, j * LANES:(j + 1) * LANES]
        for j in range(n_slabs):
            sl = slice(j * LANES, (j + 1) * LANES)
            acc = cb_ref[:, sl]
            for k in range(taps):
                acc = acc + xpad_ref[j, pl.ds(r0 + SUBLANES - (taps - 1) + k, half, stride=1), :] * cw[k:k + 1, sl]
            xbc_ref[rows, sl] = _silu(acc)
        if r0 + half == tm:
            tail_ref[...] = raw[half - SUBLANES:, :]
    xpad_ref[:, 0:SUBLANES, :] = xpad_ref[:, pl.ds(tm, SUBLANES), :]


def _inproj_prompt(x, g, w_pad, cw, cb, dtb, *, cuts, q_scale, tm, seq):
    n, d = x.shape
    wp = w_pad.shape[1]
    c0, c1, c2, c3, c4 = cuts
    widths = (c0, c1 - c0, c2 - c1, c3 - c2, c4 - c3, wp - c4)
    d_xbc = c4 - c3
    per_seq = seq // tm
    row = lambda w: pl.BlockSpec((tm, w), lambda i: (i, 0))
    feat = lambda w: pl.BlockSpec((None, w, tm), lambda i: (i // per_seq, 0, i % per_seq))
    out_specs = [row(w) for w in widths] + [feat(widths[1]), feat(widths[2]),
                                            pl.BlockSpec((None, SUBLANES, d_xbc), lambda i: (i // per_seq, 0, 0))]
    out_shape = [jax.ShapeDtypeStruct((n, w), F32) for w in widths] + [
        jax.ShapeDtypeStruct((n // seq, widths[1], seq), F32),
        jax.ShapeDtypeStruct((n // seq, widths[2], seq), F32),
        jax.ShapeDtypeStruct((n // seq, SUBLANES, d_xbc), F32)]
    return pl.pallas_call(
        functools.partial(_inproj_prompt_body, cuts=cuts, q_scale=q_scale, per_seq=per_seq),
        grid=(n // tm,),
        in_specs=[row(d), _const_spec((1, d)), _const_spec((d, wp)),
                  _const_spec(cw.shape), _const_spec(cb.shape), _const_spec(dtb.shape)],
        out_specs=out_specs,
        out_shape=out_shape,
        scratch_shapes=[pltpu.VMEM((d_xbc // LANES, SUBLANES + tm, LANES), F32)],
        compiler_params=_cparams("arbitrary"),
        name="inproj_prompt",
    )(x, g, w_pad, cw, cb, dtb)


def _t5_bucket(dist, num_buckets):
    max_exact = num_buckets // 2
    d = jnp.maximum(dist, 0)
    df = jnp.maximum(d, 1).astype(F32)
    large = max_exact + (jnp.log(df / max_exact) / math.log(MAX_DISTANCE / max_exact)
                         * (num_buckets - max_exact)).astype(jnp.int32)
    large = jnp.minimum(large, num_buckets - 1)
    return jnp.where(d < max_exact, d, large)


def _bias_body(rb_ref, idxp_ref, idxs_ref, cnt_ref, bp_ref, bs_ref, *, n_buckets, n_heads):
    n_pat = idxp_ref.shape[0]
    blk = idxp_ref.shape[1]
    qi = lax.broadcasted_iota(jnp.int32, (blk, 2 * blk), 0)
    kj = lax.broadcasted_iota(jnp.int32, (blk, 2 * blk), 1)
    delta = qi + blk - kj
    band = (delta >= 0) & (delta <= N_BACK)

    def lookup(idx, h):
        acc = jnp.zeros(idx.shape, F32)
        for c in range(n_buckets):
            acc = jnp.where(idx == c, rb_ref[c, h], acc)
        return acc

    for h in range(n_heads):
        for p in range(n_pat):
            bp_ref[p, h] = jnp.where(band, lookup(idxp_ref[p], h), NEG_INF)
        bs_ref[h:h + 1, :] = jnp.where(cnt_ref[...] > 0.0, lookup(idxs_ref[...], h), NEG_INF)


def _bias_tables(rel_bias, w_buf):
    n_buckets, n_heads = rel_bias.shape
    blk = N_BACK
    n_pat = len(DILATIONS)
    qi = jnp.arange(blk)[:, None]
    kj = jnp.arange(2 * blk)[None, :]
    delta = qi + blk - kj
    idx_p = jnp.stack([_t5_bucket(delta * d, n_buckets) for d in DILATIONS]).astype(jnp.int32)
    dist = jnp.concatenate([w_buf - jnp.arange(w_buf), jnp.zeros((LANES,), jnp.int32)])
    idx_s = _t5_bucket(dist, n_buckets).astype(jnp.int32)[None, :]
    count = sum(((dist % d == 0) & (dist <= d * N_BACK)).astype(F32) for d in DILATIONS)[None, :]
    vm = pl.BlockSpec(memory_space=pltpu.VMEM)
    bias_p, bias_s = pl.pallas_call(
        functools.partial(_bias_body, n_buckets=n_buckets, n_heads=n_heads),
        in_specs=[pl.BlockSpec(memory_space=pltpu.SMEM), vm, vm, vm],
        out_specs=[vm, vm],
        out_shape=[jax.ShapeDtypeStruct((n_pat, n_heads, blk, 2 * blk), F32),
                   jax.ShapeDtypeStruct((n_heads, w_buf + LANES), F32)],
        name="bias_tables",
    )(rel_bias, idx_p, idx_s, count)
    return bias_p, bias_s, count


def _attn_prompt_body(q_ref, k_ref, v_ref, bias_ref, o_ref, acc2, l2, m2, acc3, l3, m3, *, hd):
    seq = q_ref.shape[0]
    blk = N_BACK
    lane = lax.broadcasted_iota(jnp.int32, (1, LANES), 1)
    head_mask = (lane < hd, lane >= hd)

    def block(qb, kb, vb, bias2):
        nq = qb.shape[0]
        q2 = jnp.concatenate([jnp.where(head_mask[h], qb, 0.0).astype(BF16) for h in range(2)], axis=0)
        s = _dot_nt(q2, kb.astype(BF16)) + bias2
        m = jnp.max(s, axis=-1, keepdims=True)
        p = jnp.exp(s - m).astype(BF16)
        v2 = jnp.concatenate(
            [jnp.concatenate([jnp.where(head_mask[h], vb, 0.0).astype(BF16),
                              jnp.broadcast_to(jnp.where(head_mask[h], 1.0, 0.0).astype(BF16), vb.shape)], axis=1)
             for h in range(2)], axis=0)
        out = _dot(jnp.concatenate([p[:nq], p[nq:]], axis=1), v2)
        m_full = jnp.where(head_mask[0], m[:nq], m[nq:])
        return out[:, :LANES], out[:, LANES:], m_full

    def cur_bias(p):
        return bias_ref[p, :, :, blk:].reshape(2 * blk, blk)

    def full_bias(p):
        return bias_ref[p].reshape(2 * blk, 2 * blk)

    d3 = DILATIONS[2]

    def pat3(g, carry):
        for u in range(UNROLL):
            rows = pl.ds(g * UNROLL + u, blk, stride=d3)
            a, l, m = block(q_ref[rows, :], k_ref[rows, :], v_ref[rows, :], cur_bias(2))
            acc3[rows, :] = a
            l3[rows, :] = l
            m3[rows, :] = m
        return carry

    lax.fori_loop(0, d3 // UNROLL, pat3, 0)

    d2 = DILATIONS[1]
    nb2 = seq // (d2 * blk)

    classes_per_body = max(1, UNROLL // nb2)

    def pat2(g, carry):
        for u in range(classes_per_body):
            r = g * classes_per_body + u
            for i in range(nb2):
                rows = pl.ds(r + d2 * blk * i, blk, stride=d2)
                if i == 0:
                    krows, bias = rows, cur_bias(1)
                else:
                    krows, bias = pl.ds(r + d2 * blk * (i - 1), 2 * blk, stride=d2), full_bias(1)
                a, l, m = block(q_ref[rows, :], k_ref[krows, :], v_ref[krows, :], bias)
                acc2[rows, :] = a
                l2[rows, :] = l
                m2[rows, :] = m
        return carry

    lax.fori_loop(0, d2 // classes_per_body, pat2, 0)

    def finish(rows, a1, l1, m1):
        a2_, l2_, m2_ = acc2[rows, :], l2[rows, :], m2[rows, :]
        a3_, l3_, m3_ = acc3[rows, :], l3[rows, :], m3[rows, :]
        mx = jnp.maximum(jnp.maximum(m1, m2_), m3_)
        w1, w2, w3 = jnp.exp(m1 - mx), jnp.exp(m2_ - mx), jnp.exp(m3_ - mx)
        num = w1 * a1 + w2 * a2_ + w3 * a3_
        den = w1 * l1 + w2 * l2_ + w3 * l3_
        o_ref[rows, :] = (num / den).astype(o_ref.dtype)

    def pat1_block(start):
        rows = pl.ds(start, blk)
        kstart = start - blk if isinstance(start, int) else pl.multiple_of(start - blk, blk)
        krows = pl.ds(kstart, 2 * blk)
        finish(rows, *block(q_ref[rows, :], k_ref[krows, :], v_ref[krows, :], full_bias(0)))

    rows0 = pl.ds(0, blk)
    finish(rows0, *block(q_ref[rows0, :], k_ref[rows0, :], v_ref[rows0, :], cur_bias(0)))
    for u in range(1, UNROLL):
        pat1_block(u * blk)

    def pat1(g, carry):
        for u in range(UNROLL):
            pat1_block(pl.multiple_of((g * UNROLL + u) * blk, blk))
        return carry

    lax.fori_loop(1, seq // (blk * UNROLL), pat1, 0)


def _attn_prompt(q, k, v, bias_p, *, hd):
    b, s, da = q.shape
    n_pairs = da // LANES
    n_pat = bias_p.shape[0]
    blk = N_BACK
    tok = pl.BlockSpec((None, s, LANES), lambda i, j: (i, 0, j))
    return pl.pallas_call(
        functools.partial(_attn_prompt_body, hd=hd),
        grid=(b, n_pairs),
        in_specs=[tok, tok, tok,
                  pl.BlockSpec((n_pat, 2, blk, 2 * blk), lambda i, j: (0, j, 0, 0))],
        out_specs=tok,
        out_shape=jax.ShapeDtypeStruct((b, s, da), BF16),
        scratch_shapes=[pltpu.VMEM((s, LANES), F32) for _ in range(6)],
        compiler_params=_cparams("parallel", "arbitrary"),
        name="attn_prompt",
    )(q, k, v, bias_p)


def _ssd_prompt_body(gate_ref, xbc_ref, dt_ref, alog_ref, dskip_ref, gssm_ref, *rest,
                     n_heads, hd, n_groups, d_state, guest):
    n_in = len(guest.operands) if guest else 0
    guest_in, (y_ref, hout_ref), guest_out, ht_ref = rest[:n_in], rest[n_in:n_in + 2], rest[n_in + 2:-1], rest[-1]
    step = pl.program_id(1)
    ts = gate_ref.shape[0]
    q = SSD_CHUNK
    d_inner = n_heads * hd
    gw = d_inner // n_groups
    pairs_per_group = gw // LANES

    @pl.when(step == 0)
    def _():
        ht_ref[...] = jnp.zeros_like(ht_ref)

    lane = lax.broadcasted_iota(jnp.int32, (1, LANES), 1)
    head_mask = (lane < hd, lane >= hd)
    row_q = lax.broadcasted_iota(jnp.int32, (q, q), 0)
    col_q = lax.broadcasted_iota(jnp.int32, (q, q), 1)
    causal = row_q >= col_q
    tri = jnp.where(causal, 1.0, 0.0).astype(BF16)

    neg_a_log2e = -jnp.exp(alog_ref[...]) * math.log2(math.e)

    def chunk(c, carry):
        t0 = pl.multiple_of(c * q, q)
        rows = pl.ds(t0, q)
        xs = xbc_ref[rows, :d_inner]
        bm = xbc_ref[rows, d_inner:d_inner + n_groups * d_state]
        cm = xbc_ref[rows, d_inner + n_groups * d_state:]

        dt = dt_ref[rows, :]
        la = dt * neg_a_log2e
        la_hi = la.astype(BF16)
        r1 = la - la_hi.astype(F32)
        la_mid = r1.astype(BF16)
        la_lo = (r1 - la_mid.astype(F32)).astype(BF16)
        a_cs = _dot(tri, la_hi) + _dot(tri, la_mid) + _dot(tri, la_lo)
        total = a_cs[q - 1:q, :]
        w_s = dt * jnp.exp2(total - a_cs)
        chunk_decay = jnp.exp2(total)
        a_cs_t, dt_t, w_t = a_cs.T, dt.T, w_s.T

        y_tiles = []
        for g in range(n_groups):
            b_g = bm[:, g * d_state:(g + 1) * d_state]
            c_g = cm[:, g * d_state:(g + 1) * d_state]
            cb = _dot_nt(c_g.astype(BF16), b_g.astype(BF16))
            b_t = b_g.T
            for jl in range(pairs_per_group):
                j = g * pairs_per_group + jl
                pair = slice(jl * LANES, (jl + 1) * LANES)
                xs_pair = xs[:, j * LANES:(j + 1) * LANES]
                ht_pair = ht_ref[g, :, pair]
                y_pair = None
                st_pair = None
                cd_pair = None
                for h in range(2):
                    e = 2 * j + h
                    col = jnp.broadcast_to(a_cs[:, e:e + 1], (q, q))
                    lmat = jnp.exp2(jnp.where(causal, col - a_cs_t[e:e + 1, :], NEG_INF))
                    col_n = col if d_state == q else jnp.broadcast_to(a_cs[:, e:e + 1], (q, d_state))
                    lhs = jnp.concatenate([(cb * lmat * dt_t[e:e + 1, :]).astype(BF16),
                                           (c_g * jnp.exp2(col_n)).astype(BF16)], axis=1)
                    xs_h = jnp.where(head_mask[h], xs_pair, 0.0).astype(BF16)
                    rhs = jnp.concatenate([xs_h, jnp.where(head_mask[h], ht_pair, 0.0).astype(BF16)], axis=0)
                    yh = _dot(lhs, rhs)
                    sth = _dot((b_t * w_t[e:e + 1, :]).astype(BF16), xs_h)
                    cdh = chunk_decay[:, e:e + 1]
                    y_pair = yh if y_pair is None else y_pair + yh
                    st_pair = sth if st_pair is None else st_pair + sth
                    cd_pair = cdh if cd_pair is None else jnp.where(head_mask[0], cd_pair, cdh)
                y_tiles.append(y_pair + dskip_ref[:, j * LANES:(j + 1) * LANES] * xs_pair)
                ht_ref[g, :, pair] = ht_pair * cd_pair + st_pair

        y = jnp.concatenate(y_tiles, axis=1)
        yg = y * gate_ref[rows, :]
        parts = []
        for g in range(n_groups):
            part = yg[:, g * gw:(g + 1) * gw]
            parts.append(part * lax.rsqrt(jnp.mean(part * part, axis=-1, keepdims=True) + EPS))
        y_ref[rows, :] = (jnp.concatenate(parts, axis=1) * gssm_ref[...]).astype(y_ref.dtype)
        return carry

    lax.fori_loop(0, ts // q, chunk, 0, unroll=min(4, ts // q))
    if guest:
        guest.body(*guest_in, *guest_out)

    @pl.when(step == pl.num_programs(1) - 1)
    def _():
        for g in range(n_groups):
            hout_ref[g * gw:(g + 1) * gw, :] = ht_ref[g].T


def _ssd_prompt(gate, xbc, dt, alog, dskip, gssm, *, n_heads, hd, n_groups, d_state, ts, guest=None):
    b, s, d_inner = gate.shape
    d_xbc = xbc.shape[-1]
    gw = d_inner // n_groups
    tok = lambda w: pl.BlockSpec((None, ts, w), lambda i, j: (i, j, 0))
    return pl.pallas_call(
        functools.partial(_ssd_prompt_body, n_heads=n_heads, hd=hd, n_groups=n_groups, d_state=d_state,
                          guest=guest),
        grid=(b, s // ts),
        in_specs=[tok(d_inner), tok(d_xbc), tok(LANES),
                  _const_spec(alog.shape), _const_spec(dskip.shape), _const_spec(gssm.shape)]
        + (guest.in_specs if guest else []),
        out_specs=[tok(d_inner), pl.BlockSpec((None, d_inner, d_state), lambda i, j: (i, 0, 0))]
        + (guest.out_specs if guest else []),
        out_shape=[jax.ShapeDtypeStruct((b, s, d_inner), BF16),
                   jax.ShapeDtypeStruct((b, d_inner, d_state), F32)] + (guest.out_shape if guest else []),
        scratch_shapes=[pltpu.VMEM((n_groups, d_state, gw), F32)],
        compiler_params=_cparams("parallel", "arbitrary"),
        name="ssd_prompt",
    )(gate, xbc, dt, alog, dskip, gssm, *(guest.operands if guest else []))


def _memkv_body(m_ref, g_ref, wk_ref, wv_ref, k_ref, v_ref):
    mn = _rms(m_ref[...], g_ref[...]).astype(BF16)
    k_ref[...] = _dot(mn, wk_ref[...])
    v_ref[...] = _dot(mn, wv_ref[...])


def _memkv(mem, g, wk, wv, *, tm):
    n, d = mem.shape
    dk = wk.shape[1]
    row = lambda w: pl.BlockSpec((tm, w), lambda i: (i, 0))
    return pl.pallas_call(
        _memkv_body,
        grid=(n // tm,),
        in_specs=[row(d), _const_spec((1, d)), _const_spec(wk.shape), _const_spec(wv.shape)],
        out_specs=[row(dk), row(dk)],
        out_shape=[jax.ShapeDtypeStruct((n, dk), F32)] * 2,
        compiler_params=_cparams("parallel"),
        name="memkv",
    )(mem, g, wk, wv)


def _post_prompt_body(x_ref, oa_ref, ys_ref, mk_ref, mv_ref, woa_ref, wos_ref, gc_ref, wcq_ref, wco_ref,
                      o_ref, *, n_heads_x):
    x2 = x_ref[...] + _dot(oa_ref[...], woa_ref[...]) + _dot(ys_ref[...], wos_ref[...])
    dx = x2.shape[1] // n_heads_x
    hq = _rms(x2, gc_ref[...]).astype(BF16)
    qx = (_dot(hq, wcq_ref[...]) * (dx ** -0.5)).astype(BF16)
    mk = mk_ref[...].astype(BF16)
    mv = mv_ref[...].astype(BF16)
    outs = []
    for h in range(n_heads_x):
        sl = slice(h * dx, (h + 1) * dx)
        s = _dot_nt(qx[:, sl], mk[:, sl])
        p = jnp.exp(s - jnp.max(s, axis=-1, keepdims=True))
        l = jnp.sum(p, axis=-1, keepdims=True)
        outs.append(_dot(p.astype(BF16), mv[:, sl]) / l)
    oc = jnp.concatenate(outs, axis=1).astype(BF16)
    o_ref[...] = x2 + _dot(oc, wco_ref[...])


def _post_prompt(x1, oa, ys, mk, mv, woa, wos, gc, wcq, wco, *, n_heads_x, tm):
    b, s, d = x1.shape
    n_mem = mk.shape[1]
    tok = lambda w: pl.BlockSpec((None, tm, w), lambda i, j: (i, j, 0))
    mem = pl.BlockSpec((None, n_mem, d), lambda i, j: (i, 0, 0))
    return pl.pallas_call(
        functools.partial(_post_prompt_body, n_heads_x=n_heads_x),
        grid=(b, s // tm),
        in_specs=[tok(d), tok(oa.shape[-1]), tok(ys.shape[-1]), mem, mem,
                  _const_spec(woa.shape), _const_spec(wos.shape), _const_spec(gc.shape),
                  _const_spec(wcq.shape), _const_spec(wco.shape)],
        out_specs=tok(d),
        out_shape=jax.ShapeDtypeStruct((b, s, d), F32),
        compiler_params=_cparams("parallel", "arbitrary"),
        name="post_prompt",
    )(x1, oa, ys, mk, mv, woa, wos, gc, wcq, wco)


def _attn_sample_body(q_ref, kn_ref, vn_ref, kt_ref, vt_ref, bias_ref, cnt_ref, o_ref, *, hd):
    da, w = kt_ref.shape
    n_heads = da // hd
    row = lax.broadcasted_iota(jnp.int32, (n_heads, da), 0)
    col = lax.broadcasted_iota(jnp.int32, (n_heads, da), 1)
    hmask = (col >= row * hd) & (col < (row + 1) * hd)
    qbd16 = jnp.where(hmask, q_ref[...], 0.0).astype(BF16)
    s = _dot(qbd16, kt_ref[...].astype(BF16)) + bias_ref[:, :w]
    kn = kn_ref[...].astype(BF16).astype(F32)
    s_new = jnp.sum(qbd16.astype(F32) * kn, axis=-1, keepdims=True) + bias_ref[:, w:w + 1]
    m = jnp.maximum(jnp.max(s, axis=-1, keepdims=True), s_new)
    e = jnp.exp(s - m) * cnt_ref[:, :w]
    e_new = jnp.exp(s_new - m) * cnt_ref[:, w:w + 1]
    den = jnp.sum(e, axis=-1, keepdims=True) + e_new
    acc = _dot_nt(e.astype(BF16), vt_ref[...].astype(BF16)) + e_new * vn_ref[...]
    o_ref[...] = jnp.sum(jnp.where(hmask, acc / den, 0.0), axis=0, keepdims=True)


def _attn_sample_guest(q, kn, vn, kct, vct, bias_s, count, *, hd, seq_of):
    bs, _, da = q.shape
    w = kct.shape[-1]
    row = _seq_spec((1, da), seq_of)
    cache = _seq_spec((da, w), seq_of)
    return Guest(functools.partial(_attn_sample_body, hd=hd),
                 (q, kn, vn, kct, vct, bias_s, count),
                 [row, row, row, cache, cache, _const_spec(bias_s.shape), _const_spec(count.shape)],
                 [jax.ShapeDtypeStruct((bs, 1, da), F32)], [row])


def _ssd_sample_body(z_ref, xbc_ref, dt_ref, cc_ref, st_ref, cw_ref, cb_ref, dtb_ref, alog_ref, dskip_ref,
                     gssm_ref, y_ref, nconv_ref, nst_ref, *, n_groups):
    n_pairs = z_ref.shape[0]
    pairs_per_group = n_pairs // n_groups
    gw = pairs_per_group * LANES
    row8 = lax.broadcasted_iota(jnp.int32, (n_pairs, 1), 0)
    neg_a = -jnp.exp(alog_ref[...])
    pad = jnp.zeros((LANES - 2 * n_pairs, LANES), F32)

    cc = cc_ref[...]
    xr = xbc_ref[...]
    pre = xr * cw_ref[CONV_WIDTH - 1] + cb_ref[...]
    for k in range(CONV_WIDTH - 1):
        pre = pre + cc[k] * cw_ref[k]
        nconv_ref[k] = cc[k + 1] if k + 1 < CONV_WIDTH - 1 else xr
    act = _silu(pre)
    xs = act[:n_pairs]
    bm = act[n_pairs:n_pairs + n_groups]
    cm = act[n_pairs + n_groups:]
    dt = _softplus(dt_ref[...] + dtb_ref[...])
    d_a = jnp.exp(dt * neg_a)
    cols = jnp.concatenate([xs * dt, d_a, pad], axis=0).T
    y = jnp.zeros((n_pairs, LANES), F32)
    for j in range(n_pairs):
        g = j // pairs_per_group
        rows = pl.ds(j * LANES, LANES)
        h_new = cols[:, n_pairs + j:n_pairs + j + 1] * st_ref[rows, :] + cols[:, j:j + 1] * bm[g:g + 1, :]
        nst_ref[rows, :] = h_new
        c_sel = jnp.where(row8 == j, cm[g:g + 1, :], 0.0).astype(BF16)
        y = y + _dot_nt(c_sel, h_new.astype(BF16))
    y = y + dskip_ref[...] * xs
    yg = y * _silu(z_ref[...])
    ss = jnp.sum(yg * yg, axis=-1, keepdims=True)
    mean = jnp.zeros_like(ss)
    for g in range(n_groups):
        in_g = (row8 >= g * pairs_per_group) & (row8 < (g + 1) * pairs_per_group)
        tot = jnp.sum(jnp.where(in_g, ss, 0.0), axis=0, keepdims=True)
        mean = jnp.where(in_g, tot / gw, mean)
    y_ref[...] = yg * lax.rsqrt(mean + EPS) * gssm_ref[...]


def _ssd_sample_guest(z, xbc, dt, cc, st, cw, cb, dtb, alog, dskip, gssm, *, n_groups, seq_of):
    seq_block = lambda a: _seq_spec(a.shape[1:], seq_of)
    consts = (cw, cb, dtb, alog, dskip, gssm)
    return Guest(functools.partial(_ssd_sample_body, n_groups=n_groups),
                 (z, xbc, dt, cc, st) + consts,
                 [seq_block(a) for a in (z, xbc, dt, cc, st)] + [_const_spec(a.shape) for a in consts],
                 [jax.ShapeDtypeStruct(a.shape, F32) for a in (z, cc, st)],
                 [seq_block(a) for a in (z, cc, st)])


def _mix_out_sample_body(x_ref, oa_ref, ys_ref, woa_ref, wos_ref, gc_ref, wcq_ref, x2_ref, qx_ref, *, scale):
    x2 = x_ref[...] + _dot(oa_ref[...].astype(BF16), woa_ref[...]) + _dot(ys_ref[...].astype(BF16), wos_ref[...])
    x2_ref[...] = x2
    qx_ref[...] = _dot(_rms(x2, gc_ref[...]).astype(BF16), wcq_ref[...]) * scale


def _mix_out_sample(x1, oa, ys, woa, wos, gc, wcq, *, scale):
    vm = pl.BlockSpec(memory_space=pltpu.VMEM)
    return pl.pallas_call(
        functools.partial(_mix_out_sample_body, scale=scale),
        in_specs=[vm] * 7,
        out_specs=[vm, vm],
        out_shape=[jax.ShapeDtypeStruct(x1.shape, F32), jax.ShapeDtypeStruct((x1.shape[0], wcq.shape[1]), F32)],
        compiler_params=pltpu.CompilerParams(vmem_limit_bytes=VMEM_LIMIT_BYTES),
        name="mix_out_sample",
    )(x1, oa, ys, woa, wos, gc, wcq)


def _cross_sample_body(q_ref, mk_ref, mv_ref, o_ref):
    s = jnp.sum(mk_ref[...] * q_ref[...][None], axis=-1, keepdims=True)
    p = jnp.exp(s - jnp.max(s, axis=0, keepdims=True))
    l = jnp.sum(p, axis=0)
    o_ref[...] = jnp.sum(p * mv_ref[...], axis=0) / l


def _cross_sample_guest(qx, mk, mv, *, seq_of):
    bs, n_mem, nh, dx = mk.shape
    row = _seq_spec((nh, dx), seq_of)
    mem = _seq_spec((n_mem, nh, dx), seq_of)
    return Guest(_cross_sample_body, (qx, mk, mv), [row, mem, mem],
                 [jax.ShapeDtypeStruct((bs, nh, dx), F32)], [row])


def _cross_out_sample_body(x_ref, oc_ref, wco_ref, o_ref):
    o_ref[...] = x_ref[...] + _dot(oc_ref[...].astype(BF16), wco_ref[...])


def _cross_out_sample(x2, oc, wco):
    vm = pl.BlockSpec(memory_space=pltpu.VMEM)
    return pl.pallas_call(
        _cross_out_sample_body,
        in_specs=[vm] * 3,
        out_specs=vm,
        out_shape=jax.ShapeDtypeStruct(x2.shape, F32),
        name="cross_out_sample",
    )(x2, oc, wco)


def _tile(n, target):
    t = min(n, target)
    while n % t or (t % SUBLANES and t != n):
        t -= 1
    return t


def kernel(x_prompt, x_sample, cache_win_k, cache_win_v, cache_conv, state_ssm, cache_mem_k, cache_mem_v, mem_prompt, rel_bias, g_ffn1, w1_gate, w1_up, w1_down, g_mix, w_in, conv_w, conv_b, dt_bias, a_log, d_skip, g_ssm, w_out, g_mem, w_ck, w_cv, g_cross, w_cq, w_co, g_ffn2, w2_gate, w2_up, w2_down, g_final):
    depth = g_ffn1.shape[0]
    b, s, d = x_prompt.shape
    bs, dec_seq, _ = x_sample.shape
    assert dec_seq == 1, "sample path handles one new token per sequence"
    n_heads_a, hd_a = cache_win_k.shape[-2:]
    d_attn = n_heads_a * hd_a
    n_heads_b, hd_b, d_state = state_ssm.shape[-3:]
    d_inner = n_heads_b * hd_b
    d_xbc = conv_w.shape[-1]
    n_groups = (d_xbc - d_inner) // (2 * d_state)
    n_mem, n_heads_x, hd_x = cache_mem_k.shape[-3:]
    w_buf = cache_win_k.shape[2]
    assert hd_a * 2 == LANES and hd_b * 2 == LANES and d_state == LANES
    assert s % (DILATIONS[-1] * N_BACK) == 0 and s >= CONV_WIDTH - 1

    cuts = (d_attn, 2 * d_attn, 3 * d_attn, 3 * d_attn + d_inner, 3 * d_attn + d_inner + d_xbc)
    d_in = w_in.shape[-1]
    d_in_pad = cuts[-1] + LANES
    n_pairs = d_inner // LANES
    xbc_rows = d_xbc // LANES

    def row(a):
        return a.reshape(1, -1)

    def per_feature(a):
        return jnp.repeat(a, hd_b).reshape(1, d_inner)

    def head_lanes(a):
        return jnp.pad(a, (0, LANES - n_heads_b)).reshape(1, LANES)

    bias_p, bias_s, count_s = _bias_tables(rel_bias, w_buf)

    yp = x_prompt.reshape(b * s, d)
    ysm = x_sample.reshape(bs, d)
    tm_p = _tile(b * s, 512)
    tm_s = _tile(bs, 512)
    assert (b * s) // tm_p == bs, "guest kernels take one sample sequence per prompt grid step"
    gfin = row(g_final)
    outs = [[] for _ in range(10)]
    for i in range(depth):
        bf = lambda a: a[i].astype(BF16)
        w1g, w1u, w1d = bf(w1_gate), bf(w1_up), bf(w1_down)
        w2g, w2u, w2d = bf(w2_gate), bf(w2_up), bf(w2_down)
        w_in_p = jnp.pad(w_in[i], ((0, 0), (0, d_in_pad - d_in))).astype(BF16)
        woa, wos = w_out[i, :d_attn].astype(BF16), w_out[i, d_attn:].astype(BF16)
        wck, wcv, wcq, wco = bf(w_ck), bf(w_cv), bf(w_cq), bf(w_co)
        last = i == depth - 1

        x1s = _ffn(ysm, row(g_ffn1[i]), w1g, w1u, w1d, gfin, final_norm=False, tm=tm_s)
        qs, ks, vs, zs, xbcs, dts = _inproj(x1s, row(g_mix[i]), w_in_p, cuts=cuts, q_scale=hd_a ** -0.5, tm=tm_s)
        r3 = lambda a: a.reshape(bs, 1, a.shape[-1])
        to_t = lambda c: c.transpose(0, 2, 3, 1).reshape(bs, d_attn, w_buf)
        ts_ssd = _tile(s, 512)
        ssd_steps = s // ts_ssd
        assert b * ssd_steps == bs, "guest kernels take one sample sequence per prompt grid step"
        in_ssd = lambda bi, si: bi * ssd_steps + si
        attn_s = _attn_sample_guest(r3(qs), r3(ks), r3(vs), to_t(cache_win_k[i]), to_t(cache_win_v[i]),
                                    bias_s, count_s, hd=hd_a, seq_of=in_ssd)
        tiles = lambda a: a.reshape(a.shape[:-1] + (a.shape[-1] // LANES, LANES))
        dt_feat = jnp.repeat(dts[:, :n_heads_b], hd_b, axis=-1)
        ssd_s = _ssd_sample_guest(
            tiles(zs), tiles(xbcs), tiles(dt_feat), tiles(cache_conv[i]),
            state_ssm[i].reshape(bs, d_inner, d_state),
            tiles(conv_w[i]), tiles(conv_b[i]), tiles(jnp.repeat(dt_bias[i], hd_b)),
            tiles(jnp.repeat(a_log[i], hd_b)), tiles(jnp.repeat(d_skip[i], hd_b)), tiles(g_ssm[i]),
            n_groups=n_groups, seq_of=in_ssd)

        x1 = _ffn(yp, row(g_ffn1[i]), w1g, w1u, w1d, gfin, final_norm=False, tm=tm_p)
        q, k, v, gate, xbc, dtp, kt, vt, xbc_tail = _inproj_prompt(
            x1, row(g_mix[i]), w_in_p, conv_w[i], row(conv_b[i]), head_lanes(dt_bias[i]),
            cuts=cuts, q_scale=hd_a ** -0.5, tm=_tile(s, 512), seq=s)
        to_seq = lambda a: a.reshape(b, s, a.shape[-1])
        oa = _attn_prompt(to_seq(q), to_seq(k), to_seq(v), bias_p, hd=hd_a)
        y_ssm, h_last, oas, ys_t, nconv, nstate = _ssd_prompt(
            to_seq(gate), to_seq(xbc), to_seq(dtp), head_lanes(a_log[i]), per_feature(d_skip[i]), row(g_ssm[i]),
            n_heads=n_heads_b, hd=hd_b, n_groups=n_groups, d_state=d_state, ts=ts_ssd, guest=attn_s + ssd_s)
        mk, mv = _memkv(mem_prompt.reshape(b * n_mem, d), row(g_mem[i]), wck, wcv, tm=_tile(b * n_mem, 512))
        x3 = _post_prompt(to_seq(x1), oa, y_ssm, mk.reshape(b, n_mem, d), mv.reshape(b, n_mem, d),
                          woa, wos, row(g_cross[i]), wcq, wco, n_heads_x=n_heads_x, tm=_tile(s, 512))
        x2s, qxs = _mix_out_sample(x1s, oas.reshape(bs, d_attn), ys_t.reshape(bs, d_inner), woa, wos,
                                   row(g_cross[i]), wcq, scale=hd_x ** -0.5)
        cross_s = _cross_sample_guest(qxs.reshape(bs, n_heads_x, hd_x), cache_mem_k[i], cache_mem_v[i],
                                      seq_of=lambda t: t)
        yp, ocs = _ffn(x3.reshape(b * s, d), row(g_ffn2[i]), w2g, w2u, w2d, gfin, final_norm=last, tm=tm_p,
                       guest=cross_s)
        keep = min(MAX_DISTANCE, s)
        from_t = lambda a: a.reshape(b, n_heads_a, hd_a, s).transpose(0, 3, 1, 2)[:, s - keep:]
        outs[0].append(from_t(kt))
        outs[1].append(from_t(vt))
        outs[2].append(xbc_tail[:, SUBLANES - (CONV_WIDTH - 1):])
        outs[3].append(h_last.reshape(b, n_heads_b, hd_b, d_state))
        outs[4].append(mk.reshape(b, n_mem, n_heads_x, hd_x))
        outs[5].append(mv.reshape(b, n_mem, n_heads_x, hd_x))

        x3s = _cross_out_sample(x2s, ocs.reshape(bs, d), wco)
        ysm = _ffn(x3s, row(g_ffn2[i]), w2g, w2u, w2d, gfin, final_norm=last, tm=tm_s)
        outs[6].append(ks.reshape(bs, 1, n_heads_a, hd_a))
        outs[7].append(vs.reshape(bs, 1, n_heads_a, hd_a))
        outs[8].append(nconv.reshape(bs, CONV_WIDTH - 1, d_xbc))
        outs[9].append(nstate.reshape(bs, n_heads_b, hd_b, d_state))

    return (yp.reshape(b, s, d), ysm.reshape(bs, 1, d)) + tuple(jnp.stack(o) for o in outs)
```

```python
import functools
import math

import jax
import jax.numpy as jnp
from jax import lax
from jax.experimental import pallas as pl
from jax.experimental.pallas import tpu as pltpu

F32 = jnp.float32
BF16 = jnp.bfloat16

EPS = 1e-6
NEG_INF = -1e30
LOG2E = math.log2(math.e)
DILATIONS = (1, 4, 16)
N_BACK = 128
UNROLL = 16
ROW_SPLIT = 2
MAX_DISTANCE = 2048
SSD_CHUNK = 128
CONV_WIDTH = 4

LANES = 128
SUBLANES = 8
VMEM_LIMIT_BYTES = 56 * 1024 * 1024

_HI = lax.Precision.HIGHEST


def _cparams(*sem):
    return pltpu.CompilerParams(dimension_semantics=sem, vmem_limit_bytes=VMEM_LIMIT_BYTES)


def _const_spec(shape):
    nd = len(shape)
    return pl.BlockSpec(shape, lambda *_: (0,) * nd, pipeline_mode=pl.Buffered(1))


def _seq_spec(tail, seq_of):
    return pl.BlockSpec((None,) + tuple(tail), lambda *g: (seq_of(*g),) + (0,) * len(tail))


def _rms(x, g):
    return x * lax.rsqrt(jnp.mean(x * x, axis=-1, keepdims=True) + EPS) * g


def _silu(x):
    h = 0.5 * x
    return h + h * jnp.tanh(h)


def _softplus(x):
    return jnp.maximum(x, 0.0) + jnp.log1p(jnp.exp(-jnp.abs(x)))


def _dot(a, b):
    return jnp.dot(a, b, preferred_element_type=F32)


def _dot_nt(a, b):
    return lax.dot_general(a, b, (((1,), (1,)), ((), ())), preferred_element_type=F32)


def _dot_tn(a, b, precision=None):
    return lax.dot_general(a, b, (((0,), (0,)), ((), ())), preferred_element_type=F32,
                           precision=precision)


class Guest:
    def __init__(self, body, operands, in_specs, out_shape, out_specs):
        self.body, self.operands, self.in_specs = body, list(operands), list(in_specs)
        self.out_shape, self.out_specs = list(out_shape), list(out_specs)

    def __add__(self, other):
        n_in, m_in, n_out = len(self.operands), len(other.operands), len(self.out_shape)

        def body(*refs):
            ins, outs = refs[:n_in + m_in], refs[n_in + m_in:]
            self.body(*ins[:n_in], *outs[:n_out])
            other.body(*ins[n_in:], *outs[n_out:])

        return Guest(body, self.operands + other.operands, self.in_specs + other.in_specs,
                     self.out_shape + other.out_shape, self.out_specs + other.out_specs)


def _ffn_body(x_ref, g_ref, wg_ref, wu_ref, wd_ref, gf_ref, *rest, final_norm, guest):
    n_in = len(guest.operands) if guest else 0
    guest_in, o_ref, guest_out = rest[:n_in], rest[n_in], rest[n_in + 1:]
    x = x_ref[...]
    xn = _rms(x, g_ref[...]).astype(BF16)
    gate = _dot(xn, wg_ref[...])
    up = _dot(xn, wu_ref[...])
    h = (_silu(gate) * up).astype(BF16)
    y = x + 0.5 * _dot(h, wd_ref[...])
    if final_norm:
        y = _rms(y, gf_ref[...])
    o_ref[...] = y
    if guest:
        guest.body(*guest_in, *guest_out)


def _ffn(x, g, wg, wu, wd, g_final, *, final_norm, tm, guest=None):
    n, d = x.shape
    ff = wg.shape[1]
    row = pl.BlockSpec((tm, d), lambda i: (i, 0))
    in_specs = [row, _const_spec((1, d)), _const_spec((d, ff)), _const_spec((d, ff)),
                _const_spec((ff, d)), _const_spec((1, d))]
    out = pl.pallas_call(
        functools.partial(_ffn_body, final_norm=final_norm, guest=guest),
        grid=(n // tm,),
        in_specs=in_specs + (guest.in_specs if guest else []),
        out_specs=[row] + (guest.out_specs if guest else []),
        out_shape=[jax.ShapeDtypeStruct((n, d), F32)] + (guest.out_shape if guest else []),
        compiler_params=_cparams("parallel"),
        name="ffn",
    )(x, g, wg, wu, wd, g_final, *(guest.operands if guest else []))
    return out if guest else out[0]


def _inproj_body(x_ref, g_ref, w_ref, q_ref, k_ref, v_ref, z_ref, xbc_ref, dt_ref, *, cuts, q_scale):
    xn = _rms(x_ref[...], g_ref[...]).astype(BF16)
    proj = _dot(xn, w_ref[...])
    c0, c1, c2, c3, c4 = cuts
    q_ref[...] = proj[:, :c0] * q_scale
    k_ref[...] = proj[:, c0:c1]
    v_ref[...] = proj[:, c1:c2]
    z_ref[...] = proj[:, c2:c3]
    xbc_ref[...] = proj[:, c3:c4]
    dt_ref[...] = proj[:, c4:]


def _inproj(x, g, w_pad, *, cuts, q_scale, tm):
    n, d = x.shape
    wp = w_pad.shape[1]
    c0, c1, c2, c3, c4 = cuts
    widths = (c0, c1 - c0, c2 - c1, c3 - c2, c4 - c3, wp - c4)
    row = lambda w: pl.BlockSpec((tm, w), lambda i: (i, 0))
    return pl.pallas_call(
        functools.partial(_inproj_body, cuts=cuts, q_scale=q_scale),
        grid=(n // tm,),
        in_specs=[row(d), _const_spec((1, d)), _const_spec((d, wp))],
        out_specs=[row(w) for w in widths],
        out_shape=[jax.ShapeDtypeStruct((n, w), F32) for w in widths],
        compiler_params=_cparams("parallel"),
        name="inproj",
    )(x, g, w_pad)


def _inproj_prompt_body(x_ref, g_ref, w_ref, cw_ref, cb_ref, dtb_ref,
                        q_ref, k_ref, v_ref, gate_ref, xbc_ref, dt_ref, kt_ref, vt_ref, tail_ref,
                        xpad_ref, *, cuts, q_scale, per_seq):
    tm = x_ref.shape[0]
    n_slabs = xpad_ref.shape[0]
    taps = cw_ref.shape[0]
    c0, c1, c2, c3, c4 = cuts
    cw = cw_ref[...]

    @pl.when(pl.program_id(0) % per_seq == 0)
    def _():
        xpad_ref[:, 0:SUBLANES, :] = jnp.zeros((n_slabs, SUBLANES, LANES), F32)

    half = tm // ROW_SPLIT
    for r0 in range(0, tm, half):
        rows = pl.ds(r0, half)
        xn = _rms(x_ref[rows, :], g_ref[...]).astype(BF16)
        raw = _dot(xn, w_ref[:, c3:c4])
        gate_ref[rows, :] = _silu(_dot(xn, w_ref[:, c2:c3]))
        qkv = _dot(xn, w_ref[:, :c2])
        q_ref[rows, :] = qkv[:, :c0] * q_scale
        k_ref[rows, :] = qkv[:, c0:c1]
        v_ref[rows, :] = qkv[:, c1:c2]
        kt_ref[:, rows] = qkv[:, c0:c1].T
        vt_ref[:, rows] = qkv[:, c1:c2].T
        dt_ref[rows, :] = _softplus(_dot(xn, w_ref[:, c4:]) + dtb_ref[...])
        for j in range(n_slabs):
            xpad_ref[j, pl.ds(SUBLANES + r0, half), :] = raw[:, j * LANES:(j + 1) * LANES]
        for j in range(n_slabs):
            sl = slice(j * LANES, (j + 1) * LANES)
            acc = cb_ref[:, sl]
            for k in range(taps):
                acc = acc + xpad_ref[j, pl.ds(r0 + SUBLANES - (taps - 1) + k, half, stride=1), :] * cw[k:k + 1, sl]
            xbc_ref[rows, sl] = _silu(acc)
        if r0 + half == tm:
            tail_ref[...] = raw[half - SUBLANES:, :]
    xpad_ref[:, 0:SUBLANES, :] = xpad_ref[:, pl.ds(tm, SUBLANES), :]


def _inproj_prompt(x, g, w_pad, cw, cb, dtb, *, cuts, q_scale, tm, seq):
    n, d = x.shape
    wp = w_pad.shape[1]
    c0, c1, c2, c3, c4 = cuts
    widths = (c0, c1 - c0, c2 - c1, c3 - c2, c4 - c3, wp - c4)
    d_xbc = c4 - c3
    per_seq = seq // tm
    row = lambda w: pl.BlockSpec((tm, w), lambda i: (i, 0))
    feat = lambda w: pl.BlockSpec((None, w, tm), lambda i: (i // per_seq, 0, i % per_seq))
    out_specs = [row(w) for w in widths] + [feat(widths[1]), feat(widths[2]),
                                            pl.BlockSpec((None, SUBLANES, d_xbc), lambda i: (i // per_seq, 0, 0))]
    out_shape = [jax.ShapeDtypeStruct((n, w), F32) for w in widths] + [
        jax.ShapeDtypeStruct((n // seq, widths[1], seq), F32),
        jax.ShapeDtypeStruct((n // seq, widths[2], seq), F32),
        jax.ShapeDtypeStruct((n // seq, SUBLANES, d_xbc), F32)]
    return pl.pallas_call(
        functools.partial(_inproj_prompt_body, cuts=cuts, q_scale=q_scale, per_seq=per_seq),
        grid=(n // tm,),
        in_specs=[row(d), _const_spec((1, d)), _const_spec((d, wp)),
                  _const_spec(cw.shape), _const_spec(cb.shape), _const_spec(dtb.shape)],
        out_specs=out_specs,
        out_shape=out_shape,
        scratch_shapes=[pltpu.VMEM((d_xbc // LANES, SUBLANES + tm, LANES), F32)],
        compiler_params=_cparams("arbitrary"),
        name="inproj_prompt",
    )(x, g, w_pad, cw, cb, dtb)


def _t5_bucket(dist, num_buckets):
    max_exact = num_buckets // 2
    d = jnp.maximum(dist, 0)
    df = jnp.maximum(d, 1).astype(F32)
    large = max_exact + (jnp.log(df / max_exact) / math.log(MAX_DISTANCE / max_exact)
                         * (num_buckets - max_exact)).astype(jnp.int32)
    large = jnp.minimum(large, num_buckets - 1)
    return jnp.where(d < max_exact, d, large)


def _bias_body(rb_ref, idxp_ref, idxs_ref, cnt_ref, bp_ref, bs_ref, *, n_buckets, n_heads):
    n_pat = idxp_ref.shape[0]
    blk = idxp_ref.shape[1]
    qi = lax.broadcasted_iota(jnp.int32, (blk, 2 * blk), 0)
    kj = lax.broadcasted_iota(jnp.int32, (blk, 2 * blk), 1)
    delta = qi + blk - kj
    band = (delta >= 0) & (delta <= N_BACK)

    def lookup(idx, h):
        acc = jnp.zeros(idx.shape, F32)
        for c in range(n_buckets):
            acc = jnp.where(idx == c, rb_ref[c, h], acc)
        return acc

    for h in range(n_heads):
        for p in range(n_pat):
            bp_ref[p, h] = jnp.where(band, lookup(idxp_ref[p], h) * LOG2E, NEG_INF)
        bs_ref[h:h + 1, :] = jnp.where(cnt_ref[...] > 0.0, lookup(idxs_ref[...], h), NEG_INF)


def _bias_tables(rel_bias, w_buf):
    n_buckets, n_heads = rel_bias.shape
    blk = N_BACK
    n_pat = len(DILATIONS)
    qi = jnp.arange(blk)[:, None]
    kj = jnp.arange(2 * blk)[None, :]
    delta = qi + blk - kj
    idx_p = jnp.stack([_t5_bucket(delta * d, n_buckets) for d in DILATIONS]).astype(jnp.int32)
    dist = jnp.concatenate([w_buf - jnp.arange(w_buf), jnp.zeros((LANES,), jnp.int32)])
    idx_s = _t5_bucket(dist, n_buckets).astype(jnp.int32)[None, :]
    count = sum(((dist % d == 0) & (dist <= d * N_BACK)).astype(F32) for d in DILATIONS)[None, :]
    vm = pl.BlockSpec(memory_space=pltpu.VMEM)
    bias_p, bias_s = pl.pallas_call(
        functools.partial(_bias_body, n_buckets=n_buckets, n_heads=n_heads),
        in_specs=[pl.BlockSpec(memory_space=pltpu.SMEM), vm, vm, vm],
        out_specs=[vm, vm],
        out_shape=[jax.ShapeDtypeStruct((n_pat, n_heads, blk, 2 * blk), F32),
                   jax.ShapeDtypeStruct((n_heads, w_buf + LANES), F32)],
        name="bias_tables",
    )(rel_bias, idx_p, idx_s, count)
    return bias_p, bias_s, count


def _attn_prompt_body(q_ref, k_ref, v_ref, bias_ref, o_ref, acc2, l2, m2, acc3, l3, m3, *, hd):
    seq = q_ref.shape[0]
    blk = N_BACK
    lane = lax.broadcasted_iota(jnp.int32, (1, LANES), 1)
    head_mask = (lane < hd, lane >= hd)

    def block(qb, kb, vb, bias2):
        nq = qb.shape[0]
        q2 = jnp.concatenate([jnp.where(head_mask[h], qb, 0.0).astype(BF16) for h in range(2)], axis=0)
        s = _dot_nt(q2, kb.astype(BF16)) + bias2
        m = jnp.max(s, axis=-1, keepdims=True)
        p = jnp.exp2(s - m).astype(BF16)
        v2 = jnp.concatenate(
            [jnp.concatenate([jnp.where(head_mask[h], vb, 0.0).astype(BF16),
                              jnp.broadcast_to(jnp.where(head_mask[h], 1.0, 0.0).astype(BF16), vb.shape)], axis=1)
             for h in range(2)], axis=0)
        out = _dot(jnp.concatenate([p[:nq], p[nq:]], axis=1), v2)
        m_full = jnp.where(head_mask[0], m[:nq], m[nq:])
        return out[:, :LANES], out[:, LANES:], m_full

    def cur_bias(p):
        return bias_ref[p, :, :, blk:].reshape(2 * blk, blk)

    def full_bias(p):
        return bias_ref[p].reshape(2 * blk, 2 * blk)

    d3 = DILATIONS[2]

    def pat3(g, carry):
        for u in range(UNROLL):
            rows = pl.ds(g * UNROLL + u, blk, stride=d3)
            a, l, m = block(q_ref[rows, :], k_ref[rows, :], v_ref[rows, :], cur_bias(2))
            acc3[rows, :] = a
            l3[rows, :] = l
            m3[rows, :] = m
        return carry

    lax.fori_loop(0, d3 // UNROLL, pat3, 0)

    d2 = DILATIONS[1]
    nb2 = seq // (d2 * blk)

    classes_per_body = max(1, UNROLL // nb2)

    def pat2(g, carry):
        for u in range(classes_per_body):
            r = g * classes_per_body + u
            for i in range(nb2):
                rows = pl.ds(r + d2 * blk * i, blk, stride=d2)
                if i == 0:
                    krows, bias = rows, cur_bias(1)
                else:
                    krows, bias = pl.ds(r + d2 * blk * (i - 1), 2 * blk, stride=d2), full_bias(1)
                a, l, m = block(q_ref[rows, :], k_ref[krows, :], v_ref[krows, :], bias)
                acc2[rows, :] = a
                l2[rows, :] = l
                m2[rows, :] = m
        return carry

    lax.fori_loop(0, d2 // classes_per_body, pat2, 0)

    def finish(rows, a1, l1, m1):
        a2_, l2_, m2_ = acc2[rows, :], l2[rows, :], m2[rows, :]
        a3_, l3_, m3_ = acc3[rows, :], l3[rows, :], m3[rows, :]
        mx = jnp.maximum(jnp.maximum(m1, m2_), m3_)
        w1, w2, w3 = jnp.exp2(m1 - mx), jnp.exp2(m2_ - mx), jnp.exp2(m3_ - mx)
        num = w1 * a1 + w2 * a2_ + w3 * a3_
        den = w1 * l1 + w2 * l2_ + w3 * l3_
        o_ref[rows, :] = (num / den).astype(o_ref.dtype)

    def pat1_block(start):
        rows = pl.ds(start, blk)
        kstart = start - blk if isinstance(start, int) else pl.multiple_of(start - blk, blk)
        krows = pl.ds(kstart, 2 * blk)
        finish(rows, *block(q_ref[rows, :], k_ref[krows, :], v_ref[krows, :], full_bias(0)))

    rows0 = pl.ds(0, blk)
    finish(rows0, *block(q_ref[rows0, :], k_ref[rows0, :], v_ref[rows0, :], cur_bias(0)))
    for u in range(1, UNROLL):
        pat1_block(u * blk)

    def pat1(g, carry):
        for u in range(UNROLL):
            pat1_block(pl.multiple_of((g * UNROLL + u) * blk, blk))
        return carry

    lax.fori_loop(1, seq // (blk * UNROLL), pat1, 0)


def _attn_prompt(q, k, v, bias_p, *, hd):
    b, s, da = q.shape
    n_pairs = da // LANES
    n_pat = bias_p.shape[0]
    blk = N_BACK
    tok = pl.BlockSpec((None, s, LANES), lambda i, j: (i, 0, j))
    return pl.pallas_call(
        functools.partial(_attn_prompt_body, hd=hd),
        grid=(b, n_pairs),
        in_specs=[tok, tok, tok,
                  pl.BlockSpec((n_pat, 2, blk, 2 * blk), lambda i, j: (0, j, 0, 0))],
        out_specs=tok,
        out_shape=jax.ShapeDtypeStruct((b, s, da), BF16),
        scratch_shapes=[pltpu.VMEM((s, LANES), F32) for _ in range(6)],
        compiler_params=_cparams("parallel", "arbitrary"),
        name="attn_prompt",
    )(q, k, v, bias_p)


def _ssd_prompt_body(gate_ref, xbc_ref, dt_ref, alog_ref, dskip_ref, gssm_ref, *rest,
                     n_heads, hd, n_groups, d_state, guest):
    n_in = len(guest.operands) if guest else 0
    guest_in, (y_ref, hout_ref), guest_out, ht_ref = rest[:n_in], rest[n_in:n_in + 2], rest[n_in + 2:-1], rest[-1]
    step = pl.program_id(1)
    ts = gate_ref.shape[0]
    q = SSD_CHUNK
    d_inner = n_heads * hd
    gw = d_inner // n_groups
    pairs_per_group = gw // LANES

    @pl.when(step == 0)
    def _():
        ht_ref[...] = jnp.zeros_like(ht_ref)

    lane = lax.broadcasted_iota(jnp.int32, (1, LANES), 1)
    head_mask = (lane < hd, lane >= hd)
    row_q = lax.broadcasted_iota(jnp.int32, (q, q), 0)
    col_q = lax.broadcasted_iota(jnp.int32, (q, q), 1)
    causal = row_q >= col_q
    tri = jnp.where(causal, 1.0, 0.0).astype(BF16)

    neg_a_log2e = -jnp.exp(alog_ref[...]) * LOG2E

    def chunk(c, carry):
        t0 = pl.multiple_of(c * q, q)
        rows = pl.ds(t0, q)
        xs = xbc_ref[rows, :d_inner]
        bm = xbc_ref[rows, d_inner:d_inner + n_groups * d_state]
        cm = xbc_ref[rows, d_inner + n_groups * d_state:]

        dt = dt_ref[rows, :]
        la = dt * neg_a_log2e
        la_hi = la.astype(BF16)
        r1 = la - la_hi.astype(F32)
        la_mid = r1.astype(BF16)
        la_lo = (r1 - la_mid.astype(F32)).astype(BF16)
        a_cs = _dot(tri, la_hi) + _dot(tri, la_mid) + _dot(tri, la_lo)
        total = a_cs[q - 1:q, :]
        w_s = dt * jnp.exp2(total - a_cs)
        chunk_decay = jnp.exp2(total)
        a_cs_t, dt_t, w_t = a_cs.T, dt.T, w_s.T

        y_tiles = []
        for g in range(n_groups):
            b_g = bm[:, g * d_state:(g + 1) * d_state]
            c_g = cm[:, g * d_state:(g + 1) * d_state]
            cb = _dot_nt(c_g.astype(BF16), b_g.astype(BF16))
            b_t = b_g.T
            for jl in range(pairs_per_group):
                j = g * pairs_per_group + jl
                pair = slice(jl * LANES, (jl + 1) * LANES)
                xs_pair = xs[:, j * LANES:(j + 1) * LANES]
                ht_pair = ht_ref[g, :, pair]
                y_pair = None
                st_pair = None
                cd_pair = None
                for h in range(2):
                    e = 2 * j + h
                    col = jnp.broadcast_to(a_cs[:, e:e + 1], (q, q))
                    lmat = jnp.exp2(jnp.where(causal, col - a_cs_t[e:e + 1, :], NEG_INF))
                    col_n = col if d_state == q else jnp.broadcast_to(a_cs[:, e:e + 1], (q, d_state))
                    lhs = jnp.concatenate([(cb * lmat * dt_t[e:e + 1, :]).astype(BF16),
                                           (c_g * jnp.exp2(col_n)).astype(BF16)], axis=1)
                    xs_h = jnp.where(head_mask[h], xs_pair, 0.0).astype(BF16)
                    rhs = jnp.concatenate([xs_h, jnp.where(head_mask[h], ht_pair, 0.0).astype(BF16)], axis=0)
                    yh = _dot(lhs, rhs)
                    sth = _dot((b_t * w_t[e:e + 1, :]).astype(BF16), xs_h)
                    cdh = chunk_decay[:, e:e + 1]
                    y_pair = yh if y_pair is None else y_pair + yh
                    st_pair = sth if st_pair is None else st_pair + sth
                    cd_pair = cdh if cd_pair is None else jnp.where(head_mask[0], cd_pair, cdh)
                y_tiles.append(y_pair + dskip_ref[:, j * LANES:(j + 1) * LANES] * xs_pair)
                ht_ref[g, :, pair] = ht_pair * cd_pair + st_pair

        y = jnp.concatenate(y_tiles, axis=1)
        yg = y * gate_ref[rows, :]
        parts = []
        for g in range(n_groups):
            part = yg[:, g * gw:(g + 1) * gw]
            parts.append(part * lax.rsqrt(jnp.mean(part * part, axis=-1, keepdims=True) + EPS))
        y_ref[rows, :] = (jnp.concatenate(parts, axis=1) * gssm_ref[...]).astype(y_ref.dtype)
        return carry

    lax.fori_loop(0, ts // q, chunk, 0, unroll=min(4, ts // q))
    if guest:
        guest.body(*guest_in, *guest_out)

    @pl.when(step == pl.num_programs(1) - 1)
    def _():
        for g in range(n_groups):
            hout_ref[g * gw:(g + 1) * gw, :] = ht_ref[g].T


def _ssd_prompt(gate, xbc, dt, alog, dskip, gssm, *, n_heads, hd, n_groups, d_state, ts, guest=None):
    b, s, d_inner = gate.shape
    d_xbc = xbc.shape[-1]
    gw = d_inner // n_groups
    tok = lambda w: pl.BlockSpec((None, ts, w), lambda i, j: (i, j, 0))
    return pl.pallas_call(
        functools.partial(_ssd_prompt_body, n_heads=n_heads, hd=hd, n_groups=n_groups, d_state=d_state,
                          guest=guest),
        grid=(b, s // ts),
        in_specs=[tok(d_inner), tok(d_xbc), tok(LANES),
                  _const_spec(alog.shape), _const_spec(dskip.shape), _const_spec(gssm.shape)]
        + (guest.in_specs if guest else []),
        out_specs=[tok(d_inner), pl.BlockSpec((None, d_inner, d_state), lambda i, j: (i, 0, 0))]
        + (guest.out_specs if guest else []),
        out_shape=[jax.ShapeDtypeStruct((b, s, d_inner), BF16),
                   jax.ShapeDtypeStruct((b, d_inner, d_state), F32)] + (guest.out_shape if guest else []),
        scratch_shapes=[pltpu.VMEM((n_groups, d_state, gw), F32)],
        compiler_params=_cparams("parallel", "arbitrary"),
        name="ssd_prompt",
    )(gate, xbc, dt, alog, dskip, gssm, *(guest.operands if guest else []))


def _memkv_body(m_ref, g_ref, wk_ref, wv_ref, k_ref, v_ref):
    mn = _rms(m_ref[...], g_ref[...]).astype(BF16)
    k_ref[...] = _dot(mn, wk_ref[...])
    v_ref[...] = _dot(mn, wv_ref[...])


def _memkv(mem, g, wk, wv, *, tm):
    n, d = mem.shape
    dk = wk.shape[1]
    row = lambda w: pl.BlockSpec((tm, w), lambda i: (i, 0))
    return pl.pallas_call(
        _memkv_body,
        grid=(n // tm,),
        in_specs=[row(d), _const_spec((1, d)), _const_spec(wk.shape), _const_spec(wv.shape)],
        out_specs=[row(dk), row(dk)],
        out_shape=[jax.ShapeDtypeStruct((n, dk), F32)] * 2,
        compiler_params=_cparams("parallel"),
        name="memkv",
    )(mem, g, wk, wv)


def _post_prompt_body(x_ref, oa_ref, ys_ref, mk_ref, mv_ref, woa_ref, wos_ref, gc_ref, wcq_ref, wco_ref,
                      o_ref, *, n_heads_x):
    x2 = x_ref[...] + _dot(oa_ref[...], woa_ref[...]) + _dot(ys_ref[...], wos_ref[...])
    dx = x2.shape[1] // n_heads_x
    hq = _rms(x2, gc_ref[...]).astype(BF16)
    qx = (_dot(hq, wcq_ref[...]) * (dx ** -0.5)).astype(BF16)
    mk = mk_ref[...].astype(BF16)
    mv = mv_ref[...].astype(BF16)
    outs = []
    for h in range(n_heads_x):
        sl = slice(h * dx, (h + 1) * dx)
        s = _dot_nt(qx[:, sl], mk[:, sl])
        p = jnp.exp(s - jnp.max(s, axis=-1, keepdims=True))
        l = jnp.sum(p, axis=-1, keepdims=True)
        outs.append(_dot(p.astype(BF16), mv[:, sl]) / l)
    oc = jnp.concatenate(outs, axis=1).astype(BF16)
    o_ref[...] = x2 + _dot(oc, wco_ref[...])


def _post_prompt(x1, oa, ys, mk, mv, woa, wos, gc, wcq, wco, *, n_heads_x, tm):
    b, s, d = x1.shape
    n_mem = mk.shape[1]
    tok = lambda w: pl.BlockSpec((None, tm, w), lambda i, j: (i, j, 0))
    mem = pl.BlockSpec((None, n_mem, d), lambda i, j: (i, 0, 0))
    return pl.pallas_call(
        functools.partial(_post_prompt_body, n_heads_x=n_heads_x),
        grid=(b, s // tm),
        in_specs=[tok(d), tok(oa.shape[-1]), tok(ys.shape[-1]), mem, mem,
                  _const_spec(woa.shape), _const_spec(wos.shape), _const_spec(gc.shape),
                  _const_spec(wcq.shape), _const_spec(wco.shape)],
        out_specs=tok(d),
        out_shape=jax.ShapeDtypeStruct((b, s, d), F32),
        compiler_params=_cparams("parallel", "arbitrary"),
        name="post_prompt",
    )(x1, oa, ys, mk, mv, woa, wos, gc, wcq, wco)


def _attn_sample_body(q_ref, kn_ref, vn_ref, kt_ref, vt_ref, bias_ref, cnt_ref, o_ref, *, hd):
    da, w = kt_ref.shape
    n_heads = da // hd
    row = lax.broadcasted_iota(jnp.int32, (n_heads, da), 0)
    col = lax.broadcasted_iota(jnp.int32, (n_heads, da), 1)
    hmask = (col >= row * hd) & (col < (row + 1) * hd)
    qbd16 = jnp.where(hmask, q_ref[...], 0.0).astype(BF16)
    s = _dot(qbd16, kt_ref[...].astype(BF16)) + bias_ref[:, :w]
    kn = kn_ref[...].astype(BF16).astype(F32)
    s_new = jnp.sum(qbd16.astype(F32) * kn, axis=-1, keepdims=True) + bias_ref[:, w:w + 1]
    m = jnp.maximum(jnp.max(s, axis=-1, keepdims=True), s_new)
    e = jnp.exp(s - m) * cnt_ref[:, :w]
    e_new = jnp.exp(s_new - m) * cnt_ref[:, w:w + 1]
    den = jnp.sum(e, axis=-1, keepdims=True) + e_new
    acc = _dot_nt(e.astype(BF16), vt_ref[...].astype(BF16)) + e_new * vn_ref[...]
    o_ref[...] = jnp.sum(jnp.where(hmask, acc / den, 0.0), axis=0, keepdims=True)


def _attn_sample_guest(q, kn, vn, kct, vct, bias_s, count, *, hd, seq_of):
    bs, _, da = q.shape
    w = kct.shape[-1]
    row = _seq_spec((1, da), seq_of)
    cache = _seq_spec((da, w), seq_of)
    return Guest(functools.partial(_attn_sample_body, hd=hd),
                 (q, kn, vn, kct, vct, bias_s, count),
                 [row, row, row, cache, cache, _const_spec(bias_s.shape), _const_spec(count.shape)],
                 [jax.ShapeDtypeStruct((bs, 1, da), F32)], [row])


def _ssd_sample_body(z_ref, xbc_ref, dt_ref, cc_ref, st_ref, cw_ref, cb_ref, dtb_ref, alog_ref, dskip_ref,
                     gssm_ref, y_ref, nconv_ref, nst_ref, *, n_groups):
    n_pairs = z_ref.shape[0]
    pairs_per_group = n_pairs // n_groups
    gw = pairs_per_group * LANES
    row8 = lax.broadcasted_iota(jnp.int32, (n_pairs, 1), 0)
    neg_a = -jnp.exp(alog_ref[...])
    pad = jnp.zeros((LANES - 2 * n_pairs, LANES), F32)

    cc = cc_ref[...]
    xr = xbc_ref[...]
    pre = xr * cw_ref[CONV_WIDTH - 1] + cb_ref[...]
    for k in range(CONV_WIDTH - 1):
        pre = pre + cc[k] * cw_ref[k]
        nconv_ref[k] = cc[k + 1] if k + 1 < CONV_WIDTH - 1 else xr
    act = _silu(pre)
    xs = act[:n_pairs]
    bm = act[n_pairs:n_pairs + n_groups]
    cm = act[n_pairs + n_groups:]
    dt = _softplus(dt_ref[...] + dtb_ref[...])
    d_a = jnp.exp(dt * neg_a)
    cols = jnp.concatenate([xs * dt, d_a, pad], axis=0).T
    y = jnp.zeros((n_pairs, LANES), F32)
    for j in range(n_pairs):
        g = j // pairs_per_group
        rows = pl.ds(j * LANES, LANES)
        h_new = cols[:, n_pairs + j:n_pairs + j + 1] * st_ref[rows, :] + cols[:, j:j + 1] * bm[g:g + 1, :]
        nst_ref[rows, :] = h_new
        c_sel = jnp.where(row8 == j, cm[g:g + 1, :], 0.0).astype(BF16)
        y = y + _dot_nt(c_sel, h_new.astype(BF16))
    y = y + dskip_ref[...] * xs
    yg = y * _silu(z_ref[...])
    ss = jnp.sum(yg * yg, axis=-1, keepdims=True)
    mean = jnp.zeros_like(ss)
    for g in range(n_groups):
        in_g = (row8 >= g * pairs_per_group) & (row8 < (g + 1) * pairs_per_group)
        tot = jnp.sum(jnp.where(in_g, ss, 0.0), axis=0, keepdims=True)
        mean = jnp.where(in_g, tot / gw, mean)
    y_ref[...] = yg * lax.rsqrt(mean + EPS) * gssm_ref[...]


def _ssd_sample_guest(z, xbc, dt, cc, st, cw, cb, dtb, alog, dskip, gssm, *, n_groups, seq_of):
    seq_block = lambda a: _seq_spec(a.shape[1:], seq_of)
    consts = (cw, cb, dtb, alog, dskip, gssm)
    return Guest(functools.partial(_ssd_sample_body, n_groups=n_groups),
                 (z, xbc, dt, cc, st) + consts,
                 [seq_block(a) for a in (z, xbc, dt, cc, st)] + [_const_spec(a.shape) for a in consts],
                 [jax.ShapeDtypeStruct(a.shape, F32) for a in (z, cc, st)],
                 [seq_block(a) for a in (z, cc, st)])


def _mix_out_sample_body(x_ref, oa_ref, ys_ref, woa_ref, wos_ref, gc_ref, wcq_ref, x2_ref, qx_ref, *, scale):
    x2 = x_ref[...] + _dot(oa_ref[...].astype(BF16), woa_ref[...]) + _dot(ys_ref[...].astype(BF16), wos_ref[...])
    x2_ref[...] = x2
    qx_ref[...] = _dot(_rms(x2, gc_ref[...]).astype(BF16), wcq_ref[...]) * scale


def _mix_out_sample(x1, oa, ys, woa, wos, gc, wcq, *, scale):
    vm = pl.BlockSpec(memory_space=pltpu.VMEM)
    return pl.pallas_call(
        functools.partial(_mix_out_sample_body, scale=scale),
        in_specs=[vm] * 7,
        out_specs=[vm, vm],
        out_shape=[jax.ShapeDtypeStruct(x1.shape, F32), jax.ShapeDtypeStruct((x1.shape[0], wcq.shape[1]), F32)],
        compiler_params=pltpu.CompilerParams(vmem_limit_bytes=VMEM_LIMIT_BYTES),
        name="mix_out_sample",
    )(x1, oa, ys, woa, wos, gc, wcq)


def _cross_sample_body(q_ref, mk_ref, mv_ref, o_ref):
    s = jnp.sum(mk_ref[...] * q_ref[...][None], axis=-1, keepdims=True)
    p = jnp.exp(s - jnp.max(s, axis=0, keepdims=True))
    l = jnp.sum(p, axis=0)
    o_ref[...] = jnp.sum(p * mv_ref[...], axis=0) / l


def _cross_sample_guest(qx, mk, mv, *, seq_of):
    bs, n_mem, nh, dx = mk.shape
    row = _seq_spec((nh, dx), seq_of)
    mem = _seq_spec((n_mem, nh, dx), seq_of)
    return Guest(_cross_sample_body, (qx, mk, mv), [row, mem, mem],
                 [jax.ShapeDtypeStruct((bs, nh, dx), F32)], [row])


def _cross_out_sample_body(x_ref, oc_ref, wco_ref, o_ref):
    o_ref[...] = x_ref[...] + _dot(oc_ref[...].astype(BF16), wco_ref[...])


def _cross_out_sample(x2, oc, wco):
    vm = pl.BlockSpec(memory_space=pltpu.VMEM)
    return pl.pallas_call(
        _cross_out_sample_body,
        in_specs=[vm] * 3,
        out_specs=vm,
        out_shape=jax.ShapeDtypeStruct(x2.shape, F32),
        name="cross_out_sample",
    )(x2, oc, wco)


def _tile(n, target):
    t = min(n, target)
    while n % t or (t % SUBLANES and t != n):
        t -= 1
    return t


def kernel(x_prompt, x_sample, cache_win_k, cache_win_v, cache_conv, state_ssm, cache_mem_k, cache_mem_v, mem_prompt, rel_bias, g_ffn1, w1_gate, w1_up, w1_down, g_mix, w_in, conv_w, conv_b, dt_bias, a_log, d_skip, g_ssm, w_out, g_mem, w_ck, w_cv, g_cross, w_cq, w_co, g_ffn2, w2_gate, w2_up, w2_down, g_final):
    depth = g_ffn1.shape[0]
    b, s, d = x_prompt.shape
    bs, dec_seq, _ = x_sample.shape
    assert dec_seq == 1, "sample path handles one new token per sequence"
    n_heads_a, hd_a = cache_win_k.shape[-2:]
    d_attn = n_heads_a * hd_a
    n_heads_b, hd_b, d_state = state_ssm.shape[-3:]
    d_inner = n_heads_b * hd_b
    d_xbc = conv_w.shape[-1]
    n_groups = (d_xbc - d_inner) // (2 * d_state)
    n_mem, n_heads_x, hd_x = cache_mem_k.shape[-3:]
    w_buf = cache_win_k.shape[2]
    assert hd_a * 2 == LANES and hd_b * 2 == LANES and d_state == LANES
    assert s % (DILATIONS[-1] * N_BACK) == 0 and s >= CONV_WIDTH - 1

    cuts = (d_attn, 2 * d_attn, 3 * d_attn, 3 * d_attn + d_inner, 3 * d_attn + d_inner + d_xbc)
    d_in = w_in.shape[-1]
    d_in_pad = cuts[-1] + LANES
    n_pairs = d_inner // LANES
    xbc_rows = d_xbc // LANES

    def row(a):
        return a.reshape(1, -1)

    def per_feature(a):
        return jnp.repeat(a, hd_b).reshape(1, d_inner)

    def head_lanes(a):
        return jnp.pad(a, (0, LANES - n_heads_b)).reshape(1, LANES)

    bias_p, bias_s, count_s = _bias_tables(rel_bias, w_buf)

    yp = x_prompt.reshape(b * s, d)
    ysm = x_sample.reshape(bs, d)
    tm_p = _tile(b * s, 512)
    tm_s = _tile(bs, 512)
    assert (b * s) // tm_p == bs, "guest kernels take one sample sequence per prompt grid step"
    gfin = row(g_final)
    outs = [[] for _ in range(10)]
    for i in range(depth):
        bf = lambda a: a[i].astype(BF16)
        w1g, w1u, w1d = bf(w1_gate), bf(w1_up), bf(w1_down)
        w2g, w2u, w2d = bf(w2_gate), bf(w2_up), bf(w2_down)
        w_in_p = jnp.pad(w_in[i], ((0, 0), (0, d_in_pad - d_in))).astype(BF16)
        woa, wos = w_out[i, :d_attn].astype(BF16), w_out[i, d_attn:].astype(BF16)
        wck, wcv, wcq, wco = bf(w_ck), bf(w_cv), bf(w_cq), bf(w_co)
        last = i == depth - 1

        x1s = _ffn(ysm, row(g_ffn1[i]), w1g, w1u, w1d, gfin, final_norm=False, tm=tm_s)
        qs, ks, vs, zs, xbcs, dts = _inproj(x1s, row(g_mix[i]), w_in_p, cuts=cuts, q_scale=hd_a ** -0.5, tm=tm_s)
        r3 = lambda a: a.reshape(bs, 1, a.shape[-1])
        to_t = lambda c: c.transpose(0, 2, 3, 1).reshape(bs, d_attn, w_buf)
        ts_ssd = _tile(s, 512)
        ssd_steps = s // ts_ssd
        assert b * ssd_steps == bs, "guest kernels take one sample sequence per prompt grid step"
        in_ssd = lambda bi, si: bi * ssd_steps + si
        attn_s = _attn_sample_guest(r3(qs), r3(ks), r3(vs), to_t(cache_win_k[i]), to_t(cache_win_v[i]),
                                    bias_s, count_s, hd=hd_a, seq_of=in_ssd)
        tiles = lambda a: a.reshape(a.shape[:-1] + (a.shape[-1] // LANES, LANES))
        dt_feat = jnp.repeat(dts[:, :n_heads_b], hd_b, axis=-1)
        ssd_s = _ssd_sample_guest(
            tiles(zs), tiles(xbcs), tiles(dt_feat), tiles(cache_conv[i]),
            state_ssm[i].reshape(bs, d_inner, d_state),
            tiles(conv_w[i]), tiles(conv_b[i]), tiles(jnp.repeat(dt_bias[i], hd_b)),
            tiles(jnp.repeat(a_log[i], hd_b)), tiles(jnp.repeat(d_skip[i], hd_b)), tiles(g_ssm[i]),
            n_groups=n_groups, seq_of=in_ssd)

        x1 = _ffn(yp, row(g_ffn1[i]), w1g, w1u, w1d, gfin, final_norm=False, tm=tm_p)
        q, k, v, gate, xbc, dtp, kt, vt, xbc_tail = _inproj_prompt(
            x1, row(g_mix[i]), w_in_p, conv_w[i], row(conv_b[i]), head_lanes(dt_bias[i]),
            cuts=cuts, q_scale=hd_a ** -0.5 * LOG2E, tm=_tile(s, 512), seq=s)
        to_seq = lambda a: a.reshape(b, s, a.shape[-1])
        oa = _attn_prompt(to_seq(q), to_seq(k), to_seq(v), bias_p, hd=hd_a)
        y_ssm, h_last, oas, ys_t, nconv, nstate = _ssd_prompt(
            to_seq(gate), to_seq(xbc), to_seq(dtp), head_lanes(a_log[i]), per_feature(d_skip[i]), row(g_ssm[i]),
            n_heads=n_heads_b, hd=hd_b, n_groups=n_groups, d_state=d_state, ts=ts_ssd, guest=attn_s + ssd_s)
        mk, mv = _memkv(mem_prompt.reshape(b * n_mem, d), row(g_mem[i]), wck, wcv, tm=_tile(b * n_mem, 512))
        x3 = _post_prompt(to_seq(x1), oa, y_ssm, mk.reshape(b, n_mem, d), mv.reshape(b, n_mem, d),
                          woa, wos, row(g_cross[i]), wcq, wco, n_heads_x=n_heads_x, tm=_tile(s, 512))
        x2s, qxs = _mix_out_sample(x1s, oas.reshape(bs, d_attn), ys_t.reshape(bs, d_inner), woa, wos,
                                   row(g_cross[i]), wcq, scale=hd_x ** -0.5)
        cross_s = _cross_sample_guest(qxs.reshape(bs, n_heads_x, hd_x), cache_mem_k[i], cache_mem_v[i],
                                      seq_of=lambda t: t)
        yp, ocs = _ffn(x3.reshape(b * s, d), row(g_ffn2[i]), w2g, w2u, w2d, gfin, final_norm=last, tm=tm_p,
                       guest=cross_s)
        keep = min(MAX_DISTANCE, s)
        from_t = lambda a: a.reshape(b, n_heads_a, hd_a, s).transpose(0, 3, 1, 2)[:, s - keep:]
        outs[0].append(from_t(kt))
        outs[1].append(from_t(vt))
        outs[2].append(xbc_tail[:, SUBLANES - (CONV_WIDTH - 1):])
        outs[3].append(h_last.reshape(b, n_heads_b, hd_b, d_state))
        outs[4].append(mk.reshape(b, n_mem, n_heads_x, hd_x))
        outs[5].append(mv.reshape(b, n_mem, n_heads_x, hd_x))

        x3s = _cross_out_sample(x2s, ocs.reshape(bs, d), wco)
        ysm = _ffn(x3s, row(g_ffn2[i]), w2g, w2u, w2d, gfin, final_norm=last, tm=tm_s)
        outs[6].append(ks.reshape(bs, 1, n_heads_a, hd_a))
        outs[7].append(vs.reshape(bs, 1, n_heads_a, hd_a))
        outs[8].append(nconv.reshape(bs, CONV_WIDTH - 1, d_xbc))
        outs[9].append(nstate.reshape(bs, n_heads_b, hd_b, d_state))

    return (yp.reshape(b, s, d), ysm.reshape(bs, 1, d)) + tuple(jnp.stack(o) for o in outs)
```

```python
import functools
import math

import jax
import jax.numpy as jnp
from jax import lax
from jax.experimental import pallas as pl
from jax.experimental.pallas import tpu as pltpu

F32 = jnp.float32
BF16 = jnp.bfloat16

EPS = 1e-6
NEG_INF = -1e30
LOG2E = math.log2(math.e)
DILATIONS = (1, 4, 16)
N_BACK = 128
UNROLL = 16
ROW_SPLIT = 4
MAX_DISTANCE = 2048
SSD_CHUNK = 128
CONV_WIDTH = 4

LANES = 128
SUBLANES = 8
VMEM_LIMIT_BYTES = 56 * 1024 * 1024

_HI = lax.Precision.HIGHEST


def _cparams(*sem):
    return pltpu.CompilerParams(dimension_semantics=sem, vmem_limit_bytes=VMEM_LIMIT_BYTES)


def _const_spec(shape):
    nd = len(shape)
    return pl.BlockSpec(shape, lambda *_: (0,) * nd, pipeline_mode=pl.Buffered(1))


def _seq_spec(tail, seq_of):
    return pl.BlockSpec((None,) + tuple(tail), lambda *g: (seq_of(*g),) + (0,) * len(tail))


def _rms(x, g):
    return x * lax.rsqrt(jnp.mean(x * x, axis=-1, keepdims=True) + EPS) * g


def _silu(x):
    h = 0.5 * x
    return h + h * jnp.tanh(h)


def _softplus(x):
    return jnp.maximum(x, 0.0) + jnp.log1p(jnp.exp(-jnp.abs(x)))


def _dot(a, b):
    return jnp.dot(a, b, preferred_element_type=F32)


def _dot_nt(a, b):
    return lax.dot_general(a, b, (((1,), (1,)), ((), ())), preferred_element_type=F32)


def _dot_tn(a, b, precision=None):
    return lax.dot_general(a, b, (((0,), (0,)), ((), ())), preferred_element_type=F32,
                           precision=precision)


class Guest:
    def __init__(self, body, operands, in_specs, out_shape, out_specs):
        self.body, self.operands, self.in_specs = body, list(operands), list(in_specs)
        self.out_shape, self.out_specs = list(out_shape), list(out_specs)

    def __add__(self, other):
        n_in, m_in, n_out = len(self.operands), len(other.operands), len(self.out_shape)

        def body(*refs):
            ins, outs = refs[:n_in + m_in], refs[n_in + m_in:]
            self.body(*ins[:n_in], *outs[:n_out])
            other.body(*ins[n_in:], *outs[n_out:])

        return Guest(body, self.operands + other.operands, self.in_specs + other.in_specs,
                     self.out_shape + other.out_shape, self.out_specs + other.out_specs)


def _ffn_body(x_ref, g_ref, wg_ref, wu_ref, wd_ref, gf_ref, *rest, final_norm, guest):
    n_in = len(guest.operands) if guest else 0
    guest_in, o_ref, guest_out = rest[:n_in], rest[n_in], rest[n_in + 1:]
    x = x_ref[...]
    xn = _rms(x, g_ref[...]).astype(BF16)
    gate = _dot(xn, wg_ref[...])
    up = _dot(xn, wu_ref[...])
    h = (_silu(gate) * up).astype(BF16)
    y = x + 0.5 * _dot(h, wd_ref[...])
    if final_norm:
        y = _rms(y, gf_ref[...])
    o_ref[...] = y
    if guest:
        guest.body(*guest_in, *guest_out)


def _ffn(x, g, wg, wu, wd, g_final, *, final_norm, tm, guest=None):
    n, d = x.shape
    ff = wg.shape[1]
    row = pl.BlockSpec((tm, d), lambda i: (i, 0))
    in_specs = [row, _const_spec((1, d)), _const_spec((d, ff)), _const_spec((d, ff)),
                _const_spec((ff, d)), _const_spec((1, d))]
    out = pl.pallas_call(
        functools.partial(_ffn_body, final_norm=final_norm, guest=guest),
        grid=(n // tm,),
        in_specs=in_specs + (guest.in_specs if guest else []),
        out_specs=[row] + (guest.out_specs if guest else []),
        out_shape=[jax.ShapeDtypeStruct((n, d), F32)] + (guest.out_shape if guest else []),
        compiler_params=_cparams("parallel"),
        name="ffn",
    )(x, g, wg, wu, wd, g_final, *(guest.operands if guest else []))
    return out if guest else out[0]


def _inproj_body(x_ref, g_ref, w_ref, q_ref, k_ref, v_ref, z_ref, xbc_ref, dt_ref, *, cuts, q_scale):
    xn = _rms(x_ref[...], g_ref[...]).astype(BF16)
    proj = _dot(xn, w_ref[...])
    c0, c1, c2, c3, c4 = cuts
    q_ref[...] = proj[:, :c0] * q_scale
    k_ref[...] = proj[:, c0:c1]
    v_ref[...] = proj[:, c1:c2]
    z_ref[...] = proj[:, c2:c3]
    xbc_ref[...] = proj[:, c3:c4]
    dt_ref[...] = proj[:, c4:]


def _inproj(x, g, w_pad, *, cuts, q_scale, tm):
    n, d = x.shape
    wp = w_pad.shape[1]
    c0, c1, c2, c3, c4 = cuts
    widths = (c0, c1 - c0, c2 - c1, c3 - c2, c4 - c3, wp - c4)
    row = lambda w: pl.BlockSpec((tm, w), lambda i: (i, 0))
    return pl.pallas_call(
        functools.partial(_inproj_body, cuts=cuts, q_scale=q_scale),
        grid=(n // tm,),
        in_specs=[row(d), _const_spec((1, d)), _const_spec((d, wp))],
        out_specs=[row(w) for w in widths],
        out_shape=[jax.ShapeDtypeStruct((n, w), F32) for w in widths],
        compiler_params=_cparams("parallel"),
        name="inproj",
    )(x, g, w_pad)


def _inproj_prompt_body(x_ref, g_ref, w_ref, cw_ref, cb_ref, dtb_ref,
                        q_ref, k_ref, v_ref, gate_ref, xbc_ref, dt_ref, kt_ref, vt_ref, tail_ref,
                        xpad_ref, *, cuts, q_scale, per_seq):
    tm = x_ref.shape[0]
    n_slabs = xpad_ref.shape[0]
    taps = cw_ref.shape[0]
    c0, c1, c2, c3, c4 = cuts
    cw = cw_ref[...]

    @pl.when(pl.program_id(0) % per_seq == 0)
    def _():
        xpad_ref[:, 0:SUBLANES, :] = jnp.zeros((n_slabs, SUBLANES, LANES), F32)

    half = tm // ROW_SPLIT
    for r0 in range(0, tm, half):
        rows = pl.ds(r0, half)
        xn = _rms(x_ref[rows, :], g_ref[...]).astype(BF16)
        raw = _dot(xn, w_ref[:, c3:c4])
        gate_ref[rows, :] = _silu(_dot(xn, w_ref[:, c2:c3]))
        qkv = _dot(xn, w_ref[:, :c2])
        q_ref[rows, :] = qkv[:, :c0] * q_scale
        k_ref[rows, :] = qkv[:, c0:c1]
        v_ref[rows, :] = qkv[:, c1:c2]
        kt_ref[:, rows] = qkv[:, c0:c1].T
        vt_ref[:, rows] = qkv[:, c1:c2].T
        dt_ref[rows, :] = _softplus(_dot(xn, w_ref[:, c4:]) + dtb_ref[...])
        for j in range(n_slabs):
            xpad_ref[j, pl.ds(SUBLANES + r0, half), :] = raw[:, j * LANES:(j + 1) * LANES]
        for j in range(n_slabs):
            sl = slice(j * LANES, (j + 1) * LANES)
            acc = cb_ref[:, sl]
            for k in range(taps):
                acc = acc + xpad_ref[j, pl.ds(r0 + SUBLANES - (taps - 1) + k, half, stride=1), :] * cw[k:k + 1, sl]
            xbc_ref[rows, sl] = _silu(acc)
        if r0 + half == tm:
            tail_ref[...] = raw[half - SUBLANES:, :]
    xpad_ref[:, 0:SUBLANES, :] = xpad_ref[:, pl.ds(tm, SUBLANES), :]


def _inproj_prompt(x, g, w_pad, cw, cb, dtb, *, cuts, q_scale, tm, seq):
    n, d = x.shape
    wp = w_pad.shape[1]
    c0, c1, c2, c3, c4 = cuts
    widths = (c0, c1 - c0, c2 - c1, c3 - c2, c4 - c3, wp - c4)
    d_xbc = c4 - c3
    per_seq = seq // tm
    row = lambda w: pl.BlockSpec((tm, w), lambda i: (i, 0))
    feat = lambda w: pl.BlockSpec((None, w, tm), lambda i: (i // per_seq, 0, i % per_seq))
    out_specs = [row(w) for w in widths] + [feat(widths[1]), feat(widths[2]),
                                            pl.BlockSpec((None, SUBLANES, d_xbc), lambda i: (i // per_seq, 0, 0))]
    out_shape = [jax.ShapeDtypeStruct((n, w), F32) for w in widths] + [
        jax.ShapeDtypeStruct((n // seq, widths[1], seq), F32),
        jax.ShapeDtypeStruct((n // seq, widths[2], seq), F32),
        jax.ShapeDtypeStruct((n // seq, SUBLANES, d_xbc), F32)]
    return pl.pallas_call(
        functools.partial(_inproj_prompt_body, cuts=cuts, q_scale=q_scale, per_seq=per_seq),
        grid=(n // tm,),
        in_specs=[row(d), _const_spec((1, d)), _const_spec((d, wp)),
                  _const_spec(cw.shape), _const_spec(cb.shape), _const_spec(dtb.shape)],
        out_specs=out_specs,
        out_shape=out_shape,
        scratch_shapes=[pltpu.VMEM((d_xbc // LANES, SUBLANES + tm, LANES), F32)],
        compiler_params=_cparams("arbitrary"),
        name="inproj_prompt",
    )(x, g, w_pad, cw, cb, dtb)


def _t5_bucket(dist, num_buckets):
    max_exact = num_buckets // 2
    d = jnp.maximum(dist, 0)
    df = jnp.maximum(d, 1).astype(F32)
    large = max_exact + (jnp.log(df / max_exact) / math.log(MAX_DISTANCE / max_exact)
                         * (num_buckets - max_exact)).astype(jnp.int32)
    large = jnp.minimum(large, num_buckets - 1)
    return jnp.where(d < max_exact, d, large)


def _bias_body(rb_ref, idxp_ref, idxs_ref, cnt_ref, bp_ref, bs_ref, *, n_buckets, n_heads):
    n_pat = idxp_ref.shape[0]
    blk = idxp_ref.shape[1]
    qi = lax.broadcasted_iota(jnp.int32, (blk, 2 * blk), 0)
    kj = lax.broadcasted_iota(jnp.int32, (blk, 2 * blk), 1)
    delta = qi + blk - kj
    band = (delta >= 0) & (delta <= N_BACK)

    def lookup(idx, h):
        acc = jnp.zeros(idx.shape, F32)
        for c in range(n_buckets):
            acc = jnp.where(idx == c, rb_ref[c, h], acc)
        return acc

    for h in range(n_heads):
        for p in range(n_pat):
            bp_ref[p, h] = jnp.where(band, lookup(idxp_ref[p], h) * LOG2E, NEG_INF)
        bs_ref[h:h + 1, :] = jnp.where(cnt_ref[...] > 0.0, lookup(idxs_ref[...], h), NEG_INF)


def _bias_tables(rel_bias, w_buf):
    n_buckets, n_heads = rel_bias.shape
    blk = N_BACK
    n_pat = len(DILATIONS)
    qi = jnp.arange(blk)[:, None]
    kj = jnp.arange(2 * blk)[None, :]
    delta = qi + blk - kj
    idx_p = jnp.stack([_t5_bucket(delta * d, n_buckets) for d in DILATIONS]).astype(jnp.int32)
    dist = jnp.concatenate([w_buf - jnp.arange(w_buf), jnp.zeros((LANES,), jnp.int32)])
    idx_s = _t5_bucket(dist, n_buckets).astype(jnp.int32)[None, :]
    count = sum(((dist % d == 0) & (dist <= d * N_BACK)).astype(F32) for d in DILATIONS)[None, :]
    vm = pl.BlockSpec(memory_space=pltpu.VMEM)
    bias_p, bias_s = pl.pallas_call(
        functools.partial(_bias_body, n_buckets=n_buckets, n_heads=n_heads),
        in_specs=[pl.BlockSpec(memory_space=pltpu.SMEM), vm, vm, vm],
        out_specs=[vm, vm],
        out_shape=[jax.ShapeDtypeStruct((n_pat, n_heads, blk, 2 * blk), F32),
                   jax.ShapeDtypeStruct((n_heads, w_buf + LANES), F32)],
        name="bias_tables",
    )(rel_bias, idx_p, idx_s, count)
    return bias_p, bias_s, count


def _attn_prompt_body(q_ref, k_ref, v_ref, bias_ref, o_ref, acc2, l2, m2, acc3, l3, m3, *, hd):
    seq = q_ref.shape[0]
    blk = N_BACK
    lane = lax.broadcasted_iota(jnp.int32, (1, LANES), 1)
    head_mask = (lane < hd, lane >= hd)

    def block(qb, kb, vb, bias2):
        nq = qb.shape[0]
        q2 = jnp.concatenate([jnp.where(head_mask[h], qb, 0.0).astype(BF16) for h in range(2)], axis=0)
        s = _dot_nt(q2, kb.astype(BF16)) + bias2
        m = jnp.max(s, axis=-1, keepdims=True)
        p = jnp.exp2(s - m).astype(BF16)
        v2 = jnp.concatenate(
            [jnp.concatenate([jnp.where(head_mask[h], vb, 0.0).astype(BF16),
                              jnp.broadcast_to(jnp.where(head_mask[h], 1.0, 0.0).astype(BF16), vb.shape)], axis=1)
             for h in range(2)], axis=0)
        out = _dot(jnp.concatenate([p[:nq], p[nq:]], axis=1), v2)
        m_full = jnp.where(head_mask[0], m[:nq], m[nq:])
        return out[:, :LANES], out[:, LANES:], m_full

    def cur_bias(p):
        return bias_ref[p, :, :, blk:].reshape(2 * blk, blk)

    def full_bias(p):
        return bias_ref[p].reshape(2 * blk, 2 * blk)

    d3 = DILATIONS[2]

    def pat3(g, carry):
        for u in range(UNROLL):
            rows = pl.ds(g * UNROLL + u, blk, stride=d3)
            a, l, m = block(q_ref[rows, :], k_ref[rows, :], v_ref[rows, :], cur_bias(2))
            acc3[rows, :] = a
            l3[rows, :] = l
            m3[rows, :] = m
        return carry

    lax.fori_loop(0, d3 // UNROLL, pat3, 0)

    d2 = DILATIONS[1]
    nb2 = seq // (d2 * blk)

    classes_per_body = max(1, UNROLL // nb2)

    def pat2(g, carry):
        for u in range(classes_per_body):
            r = g * classes_per_body + u
            for i in range(nb2):
                rows = pl.ds(r + d2 * blk * i, blk, stride=d2)
                if i == 0:
                    krows, bias = rows, cur_bias(1)
                else:
                    krows, bias = pl.ds(r + d2 * blk * (i - 1), 2 * blk, stride=d2), full_bias(1)
                a, l, m = block(q_ref[rows, :], k_ref[krows, :], v_ref[krows, :], bias)
                acc2[rows, :] = a
                l2[rows, :] = l
                m2[rows, :] = m
        return carry

    lax.fori_loop(0, d2 // classes_per_body, pat2, 0)

    def finish(rows, a1, l1, m1):
        a2_, l2_, m2_ = acc2[rows, :], l2[rows, :], m2[rows, :]
        a3_, l3_, m3_ = acc3[rows, :], l3[rows, :], m3[rows, :]
        mx = jnp.maximum(jnp.maximum(m1, m2_), m3_)
        w1, w2, w3 = jnp.exp2(m1 - mx), jnp.exp2(m2_ - mx), jnp.exp2(m3_ - mx)
        num = w1 * a1 + w2 * a2_ + w3 * a3_
        den = w1 * l1 + w2 * l2_ + w3 * l3_
        o_ref[rows, :] = (num / den).astype(o_ref.dtype)

    def pat1_block(start):
        rows = pl.ds(start, blk)
        kstart = start - blk if isinstance(start, int) else pl.multiple_of(start - blk, blk)
        krows = pl.ds(kstart, 2 * blk)
        finish(rows, *block(q_ref[rows, :], k_ref[krows, :], v_ref[krows, :], full_bias(0)))

    rows0 = pl.ds(0, blk)
    finish(rows0, *block(q_ref[rows0, :], k_ref[rows0, :], v_ref[rows0, :], cur_bias(0)))
    for u in range(1, UNROLL):
        pat1_block(u * blk)

    def pat1(g, carry):
        for u in range(UNROLL):
            pat1_block(pl.multiple_of((g * UNROLL + u) * blk, blk))
        return carry

    lax.fori_loop(1, seq // (blk * UNROLL), pat1, 0)


def _attn_prompt(q, k, v, bias_p, *, hd):
    b, s, da = q.shape
    n_pairs = da // LANES
    n_pat = bias_p.shape[0]
    blk = N_BACK
    tok = pl.BlockSpec((None, s, LANES), lambda i, j: (i, 0, j))
    return pl.pallas_call(
        functools.partial(_attn_prompt_body, hd=hd),
        grid=(b, n_pairs),
        in_specs=[tok, tok, tok,
                  pl.BlockSpec((n_pat, 2, blk, 2 * blk), lambda i, j: (0, j, 0, 0))],
        out_specs=tok,
        out_shape=jax.ShapeDtypeStruct((b, s, da), BF16),
        scratch_shapes=[pltpu.VMEM((s, LANES), F32) for _ in range(6)],
        compiler_params=_cparams("parallel", "arbitrary"),
        name="attn_prompt",
    )(q, k, v, bias_p)


def _ssd_prompt_body(gate_ref, xbc_ref, dt_ref, alog_ref, dskip_ref, gssm_ref, *rest,
                     n_heads, hd, n_groups, d_state, guest):
    n_in = len(guest.operands) if guest else 0
    guest_in, (y_ref, hout_ref), guest_out, ht_ref = rest[:n_in], rest[n_in:n_in + 2], rest[n_in + 2:-1], rest[-1]
    step = pl.program_id(1)
    ts = gate_ref.shape[0]
    q = SSD_CHUNK
    d_inner = n_heads * hd
    gw = d_inner // n_groups
    pairs_per_group = gw // LANES

    @pl.when(step == 0)
    def _():
        ht_ref[...] = jnp.zeros_like(ht_ref)

    lane = lax.broadcasted_iota(jnp.int32, (1, LANES), 1)
    head_mask = (lane < hd, lane >= hd)
    row_q = lax.broadcasted_iota(jnp.int32, (q, q), 0)
    col_q = lax.broadcasted_iota(jnp.int32, (q, q), 1)
    causal = row_q >= col_q
    tri = jnp.where(causal, 1.0, 0.0).astype(BF16)

    neg_a_log2e = -jnp.exp(alog_ref[...]) * LOG2E

    def chunk(c, carry):
        t0 = pl.multiple_of(c * q, q)
        rows = pl.ds(t0, q)
        xs = xbc_ref[rows, :d_inner]
        bm = xbc_ref[rows, d_inner:d_inner + n_groups * d_state]
        cm = xbc_ref[rows, d_inner + n_groups * d_state:]

        dt = dt_ref[rows, :]
        la = dt * neg_a_log2e
        la_hi = la.astype(BF16)
        r1 = la - la_hi.astype(F32)
        la_mid = r1.astype(BF16)
        la_lo = (r1 - la_mid.astype(F32)).astype(BF16)
        a_cs = _dot(tri, la_hi) + _dot(tri, la_mid) + _dot(tri, la_lo)
        total = a_cs[q - 1:q, :]
        w_s = dt * jnp.exp2(total - a_cs)
        chunk_decay = jnp.exp2(total)
        a_cs_t, dt_t, w_t = a_cs.T, dt.T, w_s.T

        y_tiles = []
        for g in range(n_groups):
            b_g = bm[:, g * d_state:(g + 1) * d_state]
            c_g = cm[:, g * d_state:(g + 1) * d_state]
            cb = _dot_nt(c_g.astype(BF16), b_g.astype(BF16))
            b_t = b_g.T
            for jl in range(pairs_per_group):
                j = g * pairs_per_group + jl
                pair = slice(jl * LANES, (jl + 1) * LANES)
                xs_pair = xs[:, j * LANES:(j + 1) * LANES]
                ht_pair = ht_ref[g, :, pair]
                y_pair = None
                st_pair = None
                cd_pair = None
                for h in range(2):
                    e = 2 * j + h
                    col = jnp.broadcast_to(a_cs[:, e:e + 1], (q, q))
                    lmat = jnp.exp2(jnp.where(causal, col - a_cs_t[e:e + 1, :], NEG_INF))
                    col_n = col if d_state == q else jnp.broadcast_to(a_cs[:, e:e + 1], (q, d_state))
                    lhs = jnp.concatenate([(cb * lmat * dt_t[e:e + 1, :]).astype(BF16),
                                           (c_g * jnp.exp2(col_n)).astype(BF16)], axis=1)
                    xs_h = jnp.where(head_mask[h], xs_pair, 0.0).astype(BF16)
                    rhs = jnp.concatenate([xs_h, jnp.where(head_mask[h], ht_pair, 0.0).astype(BF16)], axis=0)
                    yh = _dot(lhs, rhs)
                    sth = _dot((b_t * w_t[e:e + 1, :]).astype(BF16), xs_h)
                    cdh = chunk_decay[:, e:e + 1]
                    y_pair = yh if y_pair is None else y_pair + yh
                    st_pair = sth if st_pair is None else st_pair + sth
                    cd_pair = cdh if cd_pair is None else jnp.where(head_mask[0], cd_pair, cdh)
                y_tiles.append(y_pair + dskip_ref[:, j * LANES:(j + 1) * LANES] * xs_pair)
                ht_ref[g, :, pair] = ht_pair * cd_pair + st_pair

        y = jnp.concatenate(y_tiles, axis=1)
        yg = y * gate_ref[rows, :]
        parts = []
        for g in range(n_groups):
            part = yg[:, g * gw:(g + 1) * gw]
            parts.append(part * lax.rsqrt(jnp.mean(part * part, axis=-1, keepdims=True) + EPS))
        y_ref[rows, :] = (jnp.concatenate(parts, axis=1) * gssm_ref[...]).astype(y_ref.dtype)
        return carry

    lax.fori_loop(0, ts // q, chunk, 0, unroll=min(4, ts // q))
    if guest:
        guest.body(*guest_in, *guest_out)

    @pl.when(step == pl.num_programs(1) - 1)
    def _():
        for g in range(n_groups):
            hout_ref[g * gw:(g + 1) * gw, :] = ht_ref[g].T


def _ssd_prompt(gate, xbc, dt, alog, dskip, gssm, *, n_heads, hd, n_groups, d_state, ts, guest=None):
    b, s, d_inner = gate.shape
    d_xbc = xbc.shape[-1]
    gw = d_inner // n_groups
    tok = lambda w: pl.BlockSpec((None, ts, w), lambda i, j: (i, j, 0))
    return pl.pallas_call(
        functools.partial(_ssd_prompt_body, n_heads=n_heads, hd=hd, n_groups=n_groups, d_state=d_state,
                          guest=guest),
        grid=(b, s // ts),
        in_specs=[tok(d_inner), tok(d_xbc), tok(LANES),
                  _const_spec(alog.shape), _const_spec(dskip.shape), _const_spec(gssm.shape)]
        + (guest.in_specs if guest else []),
        out_specs=[tok(d_inner), pl.BlockSpec((None, d_inner, d_state), lambda i, j: (i, 0, 0))]
        + (guest.out_specs if guest else []),
        out_shape=[jax.ShapeDtypeStruct((b, s, d_inner), BF16),
                   jax.ShapeDtypeStruct((b, d_inner, d_state), F32)] + (guest.out_shape if guest else []),
        scratch_shapes=[pltpu.VMEM((n_groups, d_state, gw), F32)],
        compiler_params=_cparams("parallel", "arbitrary"),
        name="ssd_prompt",
    )(gate, xbc, dt, alog, dskip, gssm, *(guest.operands if guest else []))


def _memkv_body(m_ref, g_ref, wk_ref, wv_ref, k_ref, v_ref, kh_ref, vh_ref):
    mn = _rms(m_ref[...], g_ref[...]).astype(BF16)
    n_heads, dx = kh_ref.shape[1:]
    for w_ref, flat_ref, heads_ref in ((wk_ref, k_ref, kh_ref), (wv_ref, v_ref, vh_ref)):
        res = _dot(mn, w_ref[...])
        flat_ref[...] = res
        for h in range(n_heads):
            heads_ref[:, h, :] = res[:, h * dx:(h + 1) * dx]


def _memkv(mem, g, wk, wv, *, n_heads, tm):
    n, d = mem.shape
    dk = wk.shape[1]
    row = pl.BlockSpec((tm, dk), lambda i: (i, 0))
    heads = pl.BlockSpec((tm, n_heads, dk // n_heads), lambda i: (i, 0, 0))
    return pl.pallas_call(
        _memkv_body,
        grid=(n // tm,),
        in_specs=[pl.BlockSpec((tm, d), lambda i: (i, 0)), _const_spec((1, d)),
                  _const_spec(wk.shape), _const_spec(wv.shape)],
        out_specs=[row, row, heads, heads],
        out_shape=[jax.ShapeDtypeStruct((n, dk), F32)] * 2
        + [jax.ShapeDtypeStruct((n, n_heads, dk // n_heads), F32)] * 2,
        compiler_params=_cparams("parallel"),
        name="memkv",
    )(mem, g, wk, wv)


def _post_prompt_body(x_ref, oa_ref, ys_ref, mk_ref, mv_ref, woa_ref, wos_ref, gc_ref, wcq_ref, wco_ref,
                      o_ref, *, n_heads_x):
    x2 = x_ref[...] + _dot(oa_ref[...], woa_ref[...]) + _dot(ys_ref[...], wos_ref[...])
    dx = x2.shape[1] // n_heads_x
    hq = _rms(x2, gc_ref[...]).astype(BF16)
    qx = (_dot(hq, wcq_ref[...]) * (dx ** -0.5)).astype(BF16)
    mk = mk_ref[...].astype(BF16)
    mv = mv_ref[...].astype(BF16)
    outs = []
    for h in range(n_heads_x):
        sl = slice(h * dx, (h + 1) * dx)
        s = _dot_nt(qx[:, sl], mk[:, sl])
        p = jnp.exp(s - jnp.max(s, axis=-1, keepdims=True))
        l = jnp.sum(p, axis=-1, keepdims=True)
        outs.append(_dot(p.astype(BF16), mv[:, sl]) / l)
    oc = jnp.concatenate(outs, axis=1).astype(BF16)
    o_ref[...] = x2 + _dot(oc, wco_ref[...])


def _post_prompt(x1, oa, ys, mk, mv, woa, wos, gc, wcq, wco, *, n_heads_x, tm):
    b, s, d = x1.shape
    n_mem = mk.shape[1]
    tok = lambda w: pl.BlockSpec((None, tm, w), lambda i, j: (i, j, 0))
    mem = pl.BlockSpec((None, n_mem, d), lambda i, j: (i, 0, 0))
    return pl.pallas_call(
        functools.partial(_post_prompt_body, n_heads_x=n_heads_x),
        grid=(b, s // tm),
        in_specs=[tok(d), tok(oa.shape[-1]), tok(ys.shape[-1]), mem, mem,
                  _const_spec(woa.shape), _const_spec(wos.shape), _const_spec(gc.shape),
                  _const_spec(wcq.shape), _const_spec(wco.shape)],
        out_specs=tok(d),
        out_shape=jax.ShapeDtypeStruct((b, s, d), F32),
        compiler_params=_cparams("parallel", "arbitrary"),
        name="post_prompt",
    )(x1, oa, ys, mk, mv, woa, wos, gc, wcq, wco)


def _attn_sample_body(q_ref, kn_ref, vn_ref, kt_ref, vt_ref, bias_ref, cnt_ref, o_ref, *, hd):
    da, w = kt_ref.shape
    n_heads = da // hd
    row = lax.broadcasted_iota(jnp.int32, (n_heads, da), 0)
    col = lax.broadcasted_iota(jnp.int32, (n_heads, da), 1)
    hmask = (col >= row * hd) & (col < (row + 1) * hd)
    qbd16 = jnp.where(hmask, q_ref[...], 0.0).astype(BF16)
    s = _dot(qbd16, kt_ref[...].astype(BF16)) + bias_ref[:, :w]
    kn = kn_ref[...].astype(BF16).astype(F32)
    s_new = jnp.sum(qbd16.astype(F32) * kn, axis=-1, keepdims=True) + bias_ref[:, w:w + 1]
    m = jnp.maximum(jnp.max(s, axis=-1, keepdims=True), s_new)
    e = jnp.exp(s - m) * cnt_ref[:, :w]
    e_new = jnp.exp(s_new - m) * cnt_ref[:, w:w + 1]
    den = jnp.sum(e, axis=-1, keepdims=True) + e_new
    acc = _dot_nt(e.astype(BF16), vt_ref[...].astype(BF16)) + e_new * vn_ref[...]
    o_ref[...] = jnp.sum(jnp.where(hmask, acc / den, 0.0), axis=0, keepdims=True)


def _attn_sample_guest(q, kn, vn, kct, vct, bias_s, count, *, hd, seq_of):
    bs, _, da = q.shape
    w = kct.shape[-1]
    row = _seq_spec((1, da), seq_of)
    cache = _seq_spec((da, w), seq_of)
    return Guest(functools.partial(_attn_sample_body, hd=hd),
                 (q, kn, vn, kct, vct, bias_s, count),
                 [row, row, row, cache, cache, _const_spec(bias_s.shape), _const_spec(count.shape)],
                 [jax.ShapeDtypeStruct((bs, 1, da), F32)], [row])


def _ssd_sample_body(z_ref, xbc_ref, dt_ref, cc_ref, st_ref, cw_ref, cb_ref, dtb_ref, alog_ref, dskip_ref,
                     gssm_ref, y_ref, nconv_ref, nst_ref, *, n_groups):
    n_pairs = z_ref.shape[0]
    pairs_per_group = n_pairs // n_groups
    gw = pairs_per_group * LANES
    row8 = lax.broadcasted_iota(jnp.int32, (n_pairs, 1), 0)
    neg_a = -jnp.exp(alog_ref[...])
    pad = jnp.zeros((LANES - 2 * n_pairs, LANES), F32)

    cc = cc_ref[...]
    xr = xbc_ref[...]
    pre = xr * cw_ref[CONV_WIDTH - 1] + cb_ref[...]
    for k in range(CONV_WIDTH - 1):
        pre = pre + cc[k] * cw_ref[k]
        nconv_ref[k] = cc[k + 1] if k + 1 < CONV_WIDTH - 1 else xr
    act = _silu(pre)
    xs = act[:n_pairs]
    bm = act[n_pairs:n_pairs + n_groups]
    cm = act[n_pairs + n_groups:]
    dt = _softplus(dt_ref[...] + dtb_ref[...])
    d_a = jnp.exp(dt * neg_a)
    cols = jnp.concatenate([xs * dt, d_a, pad], axis=0).T
    y = jnp.zeros((n_pairs, LANES), F32)
    for j in range(n_pairs):
        g = j // pairs_per_group
        rows = pl.ds(j * LANES, LANES)
        h_new = cols[:, n_pairs + j:n_pairs + j + 1] * st_ref[rows, :] + cols[:, j:j + 1] * bm[g:g + 1, :]
        nst_ref[rows, :] = h_new
        c_sel = jnp.where(row8 == j, cm[g:g + 1, :], 0.0).astype(BF16)
        y = y + _dot_nt(c_sel, h_new.astype(BF16))
    y = y + dskip_ref[...] * xs
    yg = y * _silu(z_ref[...])
    ss = jnp.sum(yg * yg, axis=-1, keepdims=True)
    mean = jnp.zeros_like(ss)
    for g in range(n_groups):
        in_g = (row8 >= g * pairs_per_group) & (row8 < (g + 1) * pairs_per_group)
        tot = jnp.sum(jnp.where(in_g, ss, 0.0), axis=0, keepdims=True)
        mean = jnp.where(in_g, tot / gw, mean)
    y_ref[...] = yg * lax.rsqrt(mean + EPS) * gssm_ref[...]


def _ssd_sample_guest(z, xbc, dt, cc, st, cw, cb, dtb, alog, dskip, gssm, *, n_groups, seq_of):
    seq_block = lambda a: _seq_spec(a.shape[1:], seq_of)
    consts = (cw, cb, dtb, alog, dskip, gssm)
    return Guest(functools.partial(_ssd_sample_body, n_groups=n_groups),
                 (z, xbc, dt, cc, st) + consts,
                 [seq_block(a) for a in (z, xbc, dt, cc, st)] + [_const_spec(a.shape) for a in consts],
                 [jax.ShapeDtypeStruct(a.shape, F32) for a in (z, cc, st)],
                 [seq_block(a) for a in (z, cc, st)])


def _mix_out_sample_body(x_ref, oa_ref, ys_ref, woa_ref, wos_ref, gc_ref, wcq_ref, x2_ref, qx_ref, *, scale):
    x2 = x_ref[...] + _dot(oa_ref[...].astype(BF16), woa_ref[...]) + _dot(ys_ref[...].astype(BF16), wos_ref[...])
    x2_ref[...] = x2
    qx_ref[...] = _dot(_rms(x2, gc_ref[...]).astype(BF16), wcq_ref[...]) * scale


def _mix_out_sample(x1, oa, ys, woa, wos, gc, wcq, *, scale):
    vm = pl.BlockSpec(memory_space=pltpu.VMEM)
    return pl.pallas_call(
        functools.partial(_mix_out_sample_body, scale=scale),
        in_specs=[vm] * 7,
        out_specs=[vm, vm],
        out_shape=[jax.ShapeDtypeStruct(x1.shape, F32), jax.ShapeDtypeStruct((x1.shape[0], wcq.shape[1]), F32)],
        compiler_params=pltpu.CompilerParams(vmem_limit_bytes=VMEM_LIMIT_BYTES),
        name="mix_out_sample",
    )(x1, oa, ys, woa, wos, gc, wcq)


def _lane_tiles(a, inverse=False):
    if inverse:
        *lead, rows, _ = a.shape
        heads = inverse
        return a.reshape(*lead, rows // heads, heads, LANES).swapaxes(-3, -2).reshape(*lead, heads, -1)
    *lead, heads, dx = a.shape
    return a.reshape(*lead, heads, dx // LANES, LANES).swapaxes(-3, -2).reshape(*lead, -1, LANES)


def _cross_sample_body(q_ref, mk_ref, mv_ref, o_ref, *, n_heads):
    rows = q_ref.shape[0]
    part = jnp.sum(mk_ref[...] * q_ref[...][None], axis=-1, keepdims=True)
    s = part
    for c in range(1, rows // n_heads):
        s = s + pltpu.roll(part, c * n_heads, 1)
    p = jnp.exp(s - jnp.max(s, axis=0, keepdims=True))
    l = jnp.sum(p, axis=0)
    o_ref[...] = jnp.sum(p * mv_ref[...], axis=0) / l


def _cross_sample_guest(qx, mk, mv, *, seq_of):
    bs, n_mem, nh, dx = mk.shape
    rows = nh * dx // LANES
    row = _seq_spec((rows, LANES), seq_of)
    mem = _seq_spec((n_mem, rows, LANES), seq_of)
    return Guest(functools.partial(_cross_sample_body, n_heads=nh),
                 (_lane_tiles(qx), _lane_tiles(mk), _lane_tiles(mv)), [row, mem, mem],
                 [jax.ShapeDtypeStruct((bs, rows, LANES), F32)], [row])


def _cross_out_sample_body(x_ref, oc_ref, wco_ref, o_ref):
    o_ref[...] = x_ref[...] + _dot(oc_ref[...].astype(BF16), wco_ref[...])


def _cross_out_sample(x2, oc, wco):
    vm = pl.BlockSpec(memory_space=pltpu.VMEM)
    return pl.pallas_call(
        _cross_out_sample_body,
        in_specs=[vm] * 3,
        out_specs=vm,
        out_shape=jax.ShapeDtypeStruct(x2.shape, F32),
        name="cross_out_sample",
    )(x2, oc, wco)


def _tile(n, target):
    t = min(n, target)
    while n % t or (t % SUBLANES and t != n):
        t -= 1
    return t


def kernel(x_prompt, x_sample, cache_win_k, cache_win_v, cache_conv, state_ssm, cache_mem_k, cache_mem_v, mem_prompt, rel_bias, g_ffn1, w1_gate, w1_up, w1_down, g_mix, w_in, conv_w, conv_b, dt_bias, a_log, d_skip, g_ssm, w_out, g_mem, w_ck, w_cv, g_cross, w_cq, w_co, g_ffn2, w2_gate, w2_up, w2_down, g_final):
    depth = g_ffn1.shape[0]
    b, s, d = x_prompt.shape
    bs, dec_seq, _ = x_sample.shape
    assert dec_seq == 1, "sample path handles one new token per sequence"
    n_heads_a, hd_a = cache_win_k.shape[-2:]
    d_attn = n_heads_a * hd_a
    n_heads_b, hd_b, d_state = state_ssm.shape[-3:]
    d_inner = n_heads_b * hd_b
    d_xbc = conv_w.shape[-1]
    n_groups = (d_xbc - d_inner) // (2 * d_state)
    n_mem, n_heads_x, hd_x = cache_mem_k.shape[-3:]
    w_buf = cache_win_k.shape[2]
    assert hd_a * 2 == LANES and hd_b * 2 == LANES and d_state == LANES
    assert s % (DILATIONS[-1] * N_BACK) == 0 and s >= CONV_WIDTH - 1

    cuts = (d_attn, 2 * d_attn, 3 * d_attn, 3 * d_attn + d_inner, 3 * d_attn + d_inner + d_xbc)
    d_in = w_in.shape[-1]
    d_in_pad = cuts[-1] + LANES
    n_pairs = d_inner // LANES
    xbc_rows = d_xbc // LANES

    def row(a):
        return a.reshape(1, -1)

    def per_feature(a):
        return jnp.repeat(a, hd_b).reshape(1, d_inner)

    def head_lanes(a):
        return jnp.pad(a, (0, LANES - n_heads_b)).reshape(1, LANES)

    bias_p, bias_s, count_s = _bias_tables(rel_bias, w_buf)

    yp = x_prompt.reshape(b * s, d)
    ysm = x_sample.reshape(bs, d)
    tm_p = _tile(b * s, 512)
    tm_s = _tile(bs, 512)
    assert (b * s) // tm_p == bs, "guest kernels take one sample sequence per prompt grid step"
    gfin = row(g_final)
    outs = [[] for _ in range(10)]
    for i in range(depth):
        bf = lambda a: a[i].astype(BF16)
        w1g, w1u, w1d = bf(w1_gate), bf(w1_up), bf(w1_down)
        w2g, w2u, w2d = bf(w2_gate), bf(w2_up), bf(w2_down)
        w_in_p = jnp.pad(w_in[i], ((0, 0), (0, d_in_pad - d_in))).astype(BF16)
        woa, wos = w_out[i, :d_attn].astype(BF16), w_out[i, d_attn:].astype(BF16)
        wck, wcv, wcq, wco = bf(w_ck), bf(w_cv), bf(w_cq), bf(w_co)
        last = i == depth - 1

        x1s = _ffn(ysm, row(g_ffn1[i]), w1g, w1u, w1d, gfin, final_norm=False, tm=tm_s)
        qs, ks, vs, zs, xbcs, dts = _inproj(x1s, row(g_mix[i]), w_in_p, cuts=cuts, q_scale=hd_a ** -0.5, tm=tm_s)
        r3 = lambda a: a.reshape(bs, 1, a.shape[-1])
        to_t = lambda c: c.transpose(0, 2, 3, 1).reshape(bs, d_attn, w_buf)
        ts_ssd = _tile(s, 512)
        ssd_steps = s // ts_ssd
        assert b * ssd_steps == bs, "guest kernels take one sample sequence per prompt grid step"
        in_ssd = lambda bi, si: bi * ssd_steps + si
        attn_s = _attn_sample_guest(r3(qs), r3(ks), r3(vs), to_t(cache_win_k[i]), to_t(cache_win_v[i]),
                                    bias_s, count_s, hd=hd_a, seq_of=in_ssd)
        tiles = lambda a: a.reshape(a.shape[:-1] + (a.shape[-1] // LANES, LANES))
        dt_feat = jnp.repeat(dts[:, :n_heads_b], hd_b, axis=-1)
        ssd_s = _ssd_sample_guest(
            tiles(zs), tiles(xbcs), tiles(dt_feat), tiles(cache_conv[i]),
            state_ssm[i].reshape(bs, d_inner, d_state),
            tiles(conv_w[i]), tiles(conv_b[i]), tiles(jnp.repeat(dt_bias[i], hd_b)),
            tiles(jnp.repeat(a_log[i], hd_b)), tiles(jnp.repeat(d_skip[i], hd_b)), tiles(g_ssm[i]),
            n_groups=n_groups, seq_of=in_ssd)

        x1 = _ffn(yp, row(g_ffn1[i]), w1g, w1u, w1d, gfin, final_norm=False, tm=tm_p)
        q, k, v, gate, xbc, dtp, kt, vt, xbc_tail = _inproj_prompt(
            x1, row(g_mix[i]), w_in_p, conv_w[i], row(conv_b[i]), head_lanes(dt_bias[i]),
            cuts=cuts, q_scale=hd_a ** -0.5 * LOG2E, tm=_tile(s, 512), seq=s)
        to_seq = lambda a: a.reshape(b, s, a.shape[-1])
        oa = _attn_prompt(to_seq(q), to_seq(k), to_seq(v), bias_p, hd=hd_a)
        y_ssm, h_last, oas, ys_t, nconv, nstate = _ssd_prompt(
            to_seq(gate), to_seq(xbc), to_seq(dtp), head_lanes(a_log[i]), per_feature(d_skip[i]), row(g_ssm[i]),
            n_heads=n_heads_b, hd=hd_b, n_groups=n_groups, d_state=d_state, ts=ts_ssd, guest=attn_s + ssd_s)
        mk, mv, mk_h, mv_h = _memkv(mem_prompt.reshape(b * n_mem, d), row(g_mem[i]), wck, wcv,
                                    n_heads=n_heads_x, tm=_tile(b * n_mem, 512))
        x3 = _post_prompt(to_seq(x1), oa, y_ssm, mk.reshape(b, n_mem, d), mv.reshape(b, n_mem, d),
                          woa, wos, row(g_cross[i]), wcq, wco, n_heads_x=n_heads_x, tm=_tile(s, 512))
        x2s, qxs = _mix_out_sample(x1s, oas.reshape(bs, d_attn), ys_t.reshape(bs, d_inner), woa, wos,
                                   row(g_cross[i]), wcq, scale=hd_x ** -0.5)
        cross_s = _cross_sample_guest(qxs.reshape(bs, n_heads_x, hd_x), cache_mem_k[i], cache_mem_v[i],
                                      seq_of=lambda t: t)
        yp, ocs = _ffn(x3.reshape(b * s, d), row(g_ffn2[i]), w2g, w2u, w2d, gfin, final_norm=last, tm=tm_p,
                       guest=cross_s)
        keep = min(MAX_DISTANCE, s)
        from_t = lambda a: a.reshape(b, n_heads_a, hd_a, s).transpose(0, 3, 1, 2)[:, s - keep:]
        outs[0].append(from_t(kt))
        outs[1].append(from_t(vt))
        outs[2].append(xbc_tail[:, SUBLANES - (CONV_WIDTH - 1):])
        outs[3].append(h_last.reshape(b, n_heads_b, hd_b, d_state))
        outs[4].append(mk_h.reshape(b, n_mem, n_heads_x, hd_x))
        outs[5].append(mv_h.reshape(b, n_mem, n_heads_x, hd_x))

        x3s = _cross_out_sample(x2s, _lane_tiles(ocs, inverse=n_heads_x).reshape(bs, d), wco)
        ysm = _ffn(x3s, row(g_ffn2[i]), w2g, w2u, w2d, gfin, final_norm=last, tm=tm_s)
        outs[6].append(ks.reshape(bs, 1, n_heads_a, hd_a))
        outs[7].append(vs.reshape(bs, 1, n_heads_a, hd_a))
        outs[8].append(nconv.reshape(bs, CONV_WIDTH - 1, d_xbc))
        outs[9].append(nstate.reshape(bs, n_heads_b, hd_b, d_state))

    return (yp.reshape(b, s, d), ysm.reshape(bs, 1, d)) + tuple(jnp.stack(o) for o in outs)
```

```python
import functools
import math

import jax
import jax.numpy as jnp
from jax import lax
from jax.experimental import pallas as pl
from jax.experimental.pallas import tpu as pltpu

F32 = jnp.float32
BF16 = jnp.bfloat16

EPS = 1e-6
NEG_INF = -1e30
LOG2E = math.log2(math.e)
DILATIONS = (1, 4, 16)
N_BACK = 128
UNROLL = 16
ROW_SPLIT = 4
MAX_DISTANCE = 2048
SSD_CHUNK = 128
CONV_WIDTH = 4

LANES = 128
SUBLANES = 8
VMEM_LIMIT_BYTES = 56 * 1024 * 1024

_HI = lax.Precision.HIGHEST


def _cparams(*sem):
    return pltpu.CompilerParams(dimension_semantics=sem, vmem_limit_bytes=VMEM_LIMIT_BYTES)


def _const_spec(shape):
    nd = len(shape)
    return pl.BlockSpec(shape, lambda *_: (0,) * nd, pipeline_mode=pl.Buffered(1))


def _seq_spec(tail, seq_of):
    return pl.BlockSpec((None,) + tuple(tail), lambda *g: (seq_of(*g),) + (0,) * len(tail))


def _rms(x, g):
    return x * lax.rsqrt(jnp.mean(x * x, axis=-1, keepdims=True) + EPS) * g


def _silu(x):
    h = 0.5 * x
    return h + h * jnp.tanh(h)


def _softplus(x):
    return jnp.maximum(x, 0.0) + jnp.log1p(jnp.exp(-jnp.abs(x)))


def _dot(a, b):
    return jnp.dot(a, b, preferred_element_type=F32)


def _dot_nt(a, b):
    return lax.dot_general(a, b, (((1,), (1,)), ((), ())), preferred_element_type=F32)


def _dot_tn(a, b, precision=None):
    return lax.dot_general(a, b, (((0,), (0,)), ((), ())), preferred_element_type=F32,
                           precision=precision)


class Guest:
    def __init__(self, body, operands, in_specs, out_shape, out_specs):
        self.body, self.operands, self.in_specs = body, list(operands), list(in_specs)
        self.out_shape, self.out_specs = list(out_shape), list(out_specs)

    def __add__(self, other):
        n_in, m_in, n_out = len(self.operands), len(other.operands), len(self.out_shape)

        def body(*refs):
            ins, outs = refs[:n_in + m_in], refs[n_in + m_in:]
            self.body(*ins[:n_in], *outs[:n_out])
            other.body(*ins[n_in:], *outs[n_out:])

        return Guest(body, self.operands + other.operands, self.in_specs + other.in_specs,
                     self.out_shape + other.out_shape, self.out_specs + other.out_specs)


def _ffn_body(x_ref, g_ref, wg_ref, wu_ref, wd_ref, gf_ref, *rest, final_norm, guest):
    n_in = len(guest.operands) if guest else 0
    guest_in, o_ref, guest_out = rest[:n_in], rest[n_in], rest[n_in + 1:]
    x = x_ref[...]
    xn = _rms(x, g_ref[...]).astype(BF16)
    gate = _dot(xn, wg_ref[...])
    up = _dot(xn, wu_ref[...])
    h = (_silu(gate) * up).astype(BF16)
    y = x + 0.5 * _dot(h, wd_ref[...])
    if final_norm:
        y = _rms(y, gf_ref[...])
    o_ref[...] = y
    if guest:
        guest.body(*guest_in, *guest_out)


def _ffn(x, g, wg, wu, wd, g_final, *, final_norm, tm, guest=None):
    n, d = x.shape
    ff = wg.shape[1]
    row = pl.BlockSpec((tm, d), lambda i: (i, 0))
    in_specs = [row, _const_spec((1, d)), _const_spec((d, ff)), _const_spec((d, ff)),
                _const_spec((ff, d)), _const_spec((1, d))]
    out = pl.pallas_call(
        functools.partial(_ffn_body, final_norm=final_norm, guest=guest),
        grid=(n // tm,),
        in_specs=in_specs + (guest.in_specs if guest else []),
        out_specs=[row] + (guest.out_specs if guest else []),
        out_shape=[jax.ShapeDtypeStruct((n, d), F32)] + (guest.out_shape if guest else []),
        compiler_params=_cparams("parallel"),
        name="ffn",
    )(x, g, wg, wu, wd, g_final, *(guest.operands if guest else []))
    return out if guest else out[0]


def _inproj_body(x_ref, g_ref, w_ref, q_ref, k_ref, v_ref, z_ref, xbc_ref, dt_ref, *, cuts, q_scale):
    xn = _rms(x_ref[...], g_ref[...]).astype(BF16)
    proj = _dot(xn, w_ref[...])
    c0, c1, c2, c3, c4 = cuts
    q_ref[...] = proj[:, :c0] * q_scale
    k_ref[...] = proj[:, c0:c1]
    v_ref[...] = proj[:, c1:c2]
    z_ref[...] = proj[:, c2:c3]
    xbc_ref[...] = proj[:, c3:c4]
    dt_ref[...] = proj[:, c4:]


def _inproj(x, g, w_pad, *, cuts, q_scale, tm):
    n, d = x.shape
    wp = w_pad.shape[1]
    c0, c1, c2, c3, c4 = cuts
    widths = (c0, c1 - c0, c2 - c1, c3 - c2, c4 - c3, wp - c4)
    row = lambda w: pl.BlockSpec((tm, w), lambda i: (i, 0))
    return pl.pallas_call(
        functools.partial(_inproj_body, cuts=cuts, q_scale=q_scale),
        grid=(n // tm,),
        in_specs=[row(d), _const_spec((1, d)), _const_spec((d, wp))],
        out_specs=[row(w) for w in widths],
        out_shape=[jax.ShapeDtypeStruct((n, w), F32) for w in widths],
        compiler_params=_cparams("parallel"),
        name="inproj",
    )(x, g, w_pad)


def _inproj_prompt_body(x_ref, g_ref, w_ref, cw_ref, cb_ref, dtb_ref,
                        q_ref, k_ref, v_ref, gate_ref, xbc_ref, dt_ref, kt_ref, vt_ref, tail_ref,
                        xpad_ref, *, cuts, q_scale, per_seq):
    tm = x_ref.shape[0]
    n_slabs = xpad_ref.shape[0]
    taps = cw_ref.shape[0]
    c0, c1, c2, c3, c4 = cuts
    cw = cw_ref[...]

    @pl.when(pl.program_id(0) % per_seq == 0)
    def _():
        xpad_ref[:, 0:SUBLANES, :] = jnp.zeros((n_slabs, SUBLANES, LANES), F32)

    half = tm // ROW_SPLIT
    for r0 in range(0, tm, half):
        rows = pl.ds(r0, half)
        xn = _rms(x_ref[rows, :], g_ref[...]).astype(BF16)
        raw = _dot(xn, w_ref[:, c3:c4])
        gate_ref[rows, :] = _silu(_dot(xn, w_ref[:, c2:c3]))
        qkv = _dot(xn, w_ref[:, :c2])
        q_ref[rows, :] = qkv[:, :c0] * q_scale
        k_ref[rows, :] = qkv[:, c0:c1]
        v_ref[rows, :] = qkv[:, c1:c2]
        kt_ref[:, rows] = qkv[:, c0:c1].T
        vt_ref[:, rows] = qkv[:, c1:c2].T
        dt_ref[rows, :] = _softplus(_dot(xn, w_ref[:, c4:]) + dtb_ref[...])
        for j in range(n_slabs):
            xpad_ref[j, pl.ds(SUBLANES + r0, half), :] = raw[:, j * LANES:(j + 1) * LANES]
        for j in range(n_slabs):
            sl = slice(j * LANES, (j + 1) * LANES)
            acc = cb_ref[:, sl]
            for k in range(taps):
                acc = acc + xpad_ref[j, pl.ds(r0 + SUBLANES - (taps - 1) + k, half, stride=1), :] * cw[k:k + 1, sl]
            xbc_ref[rows, sl] = _silu(acc)
        if r0 + half == tm:
            tail_ref[...] = raw[half - SUBLANES:, :]
    xpad_ref[:, 0:SUBLANES, :] = xpad_ref[:, pl.ds(tm, SUBLANES), :]


def _inproj_prompt(x, g, w_pad, cw, cb, dtb, *, cuts, q_scale, tm, seq):
    n, d = x.shape
    wp = w_pad.shape[1]
    c0, c1, c2, c3, c4 = cuts
    widths = (c0, c1 - c0, c2 - c1, c3 - c2, c4 - c3, wp - c4)
    d_xbc = c4 - c3
    per_seq = seq // tm
    row = lambda w: pl.BlockSpec((tm, w), lambda i: (i, 0))
    feat = lambda w: pl.BlockSpec((None, w, tm), lambda i: (i // per_seq, 0, i % per_seq))
    out_specs = [row(w) for w in widths] + [feat(widths[1]), feat(widths[2]),
                                            pl.BlockSpec((None, SUBLANES, d_xbc), lambda i: (i // per_seq, 0, 0))]
    out_shape = [jax.ShapeDtypeStruct((n, w), F32) for w in widths] + [
        jax.ShapeDtypeStruct((n // seq, widths[1], seq), F32),
        jax.ShapeDtypeStruct((n // seq, widths[2], seq), F32),
        jax.ShapeDtypeStruct((n // seq, SUBLANES, d_xbc), F32)]
    return pl.pallas_call(
        functools.partial(_inproj_prompt_body, cuts=cuts, q_scale=q_scale, per_seq=per_seq),
        grid=(n // tm,),
        in_specs=[row(d), _const_spec((1, d)), _const_spec((d, wp)),
                  _const_spec(cw.shape), _const_spec(cb.shape), _const_spec(dtb.shape)],
        out_specs=out_specs,
        out_shape=out_shape,
        scratch_shapes=[pltpu.VMEM((d_xbc // LANES, SUBLANES + tm, LANES), F32)],
        compiler_params=_cparams("arbitrary"),
        name="inproj_prompt",
    )(x, g, w_pad, cw, cb, dtb)


def _t5_bucket(dist, num_buckets):
    max_exact = num_buckets // 2
    d = jnp.maximum(dist, 0)
    df = jnp.maximum(d, 1).astype(F32)
    large = max_exact + (jnp.log(df / max_exact) / math.log(MAX_DISTANCE / max_exact)
                         * (num_buckets - max_exact)).astype(jnp.int32)
    large = jnp.minimum(large, num_buckets - 1)
    return jnp.where(d < max_exact, d, large)


def _bias_body(rb_ref, idxp_ref, idxs_ref, cnt_ref, bp_ref, bs_ref, *, n_buckets, n_heads):
    n_pat = idxp_ref.shape[0]
    blk = idxp_ref.shape[1]
    qi = lax.broadcasted_iota(jnp.int32, (blk, 2 * blk), 0)
    kj = lax.broadcasted_iota(jnp.int32, (blk, 2 * blk), 1)
    delta = qi + blk - kj
    band = (delta >= 0) & (delta <= N_BACK)

    def lookup(idx, h):
        acc = jnp.zeros(idx.shape, F32)
        for c in range(n_buckets):
            acc = jnp.where(idx == c, rb_ref[c, h], acc)
        return acc

    for h in range(n_heads):
        for p in range(n_pat):
            bp_ref[p, h] = jnp.where(band, lookup(idxp_ref[p], h) * LOG2E, NEG_INF)
        bs_ref[h:h + 1, :] = jnp.where(cnt_ref[...] > 0.0, lookup(idxs_ref[...], h), NEG_INF)


def _bias_tables(rel_bias, w_buf):
    n_buckets, n_heads = rel_bias.shape
    blk = N_BACK
    n_pat = len(DILATIONS)
    qi = jnp.arange(blk)[:, None]
    kj = jnp.arange(2 * blk)[None, :]
    delta = qi + blk - kj
    idx_p = jnp.stack([_t5_bucket(delta * d, n_buckets) for d in DILATIONS]).astype(jnp.int32)
    dist = jnp.concatenate([w_buf - jnp.arange(w_buf), jnp.zeros((LANES,), jnp.int32)])
    idx_s = _t5_bucket(dist, n_buckets).astype(jnp.int32)[None, :]
    count = sum(((dist % d == 0) & (dist <= d * N_BACK)).astype(F32) for d in DILATIONS)[None, :]
    vm = pl.BlockSpec(memory_space=pltpu.VMEM)
    bias_p, bias_s = pl.pallas_call(
        functools.partial(_bias_body, n_buckets=n_buckets, n_heads=n_heads),
        in_specs=[pl.BlockSpec(memory_space=pltpu.SMEM), vm, vm, vm],
        out_specs=[vm, vm],
        out_shape=[jax.ShapeDtypeStruct((n_pat, n_heads, blk, 2 * blk), F32),
                   jax.ShapeDtypeStruct((n_heads, w_buf + LANES), F32)],
        name="bias_tables",
    )(rel_bias, idx_p, idx_s, count)
    return bias_p, bias_s, count


def _attn_prompt_body(q_ref, k_ref, v_ref, bias_ref, o_ref, acc2, l2, m2, acc3, l3, m3, *, hd):
    seq = q_ref.shape[0]
    blk = N_BACK
    lane = lax.broadcasted_iota(jnp.int32, (1, LANES), 1)
    head_mask = (lane < hd, lane >= hd)

    def block(qb, kb, vb, bias2):
        nq = qb.shape[0]
        q2 = jnp.concatenate([jnp.where(head_mask[h], qb, 0.0).astype(BF16) for h in range(2)], axis=0)
        s = _dot_nt(q2, kb.astype(BF16)) + bias2
        m = jnp.max(s, axis=-1, keepdims=True)
        p = jnp.exp2(s - m).astype(BF16)
        v2 = jnp.concatenate(
            [jnp.concatenate([jnp.where(head_mask[h], vb, 0.0).astype(BF16),
                              jnp.broadcast_to(jnp.where(head_mask[h], 1.0, 0.0).astype(BF16), vb.shape)], axis=1)
             for h in range(2)], axis=0)
        out = _dot(jnp.concatenate([p[:nq], p[nq:]], axis=1), v2)
        m_full = jnp.where(head_mask[0], m[:nq], m[nq:])
        return out[:, :LANES], out[:, LANES:], m_full

    def cur_bias(p):
        return bias_ref[p, :, :, blk:].reshape(2 * blk, blk)

    def full_bias(p):
        return bias_ref[p].reshape(2 * blk, 2 * blk)

    d3 = DILATIONS[2]

    def pat3(g, carry):
        for u in range(UNROLL):
            rows = pl.ds(g * UNROLL + u, blk, stride=d3)
            a, l, m = block(q_ref[rows, :], k_ref[rows, :], v_ref[rows, :], cur_bias(2))
            acc3[rows, :] = a
            l3[rows, :] = l
            m3[rows, :] = m
        return carry

    lax.fori_loop(0, d3 // UNROLL, pat3, 0)

    d2 = DILATIONS[1]
    nb2 = seq // (d2 * blk)

    classes_per_body = max(1, UNROLL // nb2)

    def pat2(g, carry):
        for u in range(classes_per_body):
            r = g * classes_per_body + u
            for i in range(nb2):
                rows = pl.ds(r + d2 * blk * i, blk, stride=d2)
                if i == 0:
                    krows, bias = rows, cur_bias(1)
                else:
                    krows, bias = pl.ds(r + d2 * blk * (i - 1), 2 * blk, stride=d2), full_bias(1)
                a, l, m = block(q_ref[rows, :], k_ref[krows, :], v_ref[krows, :], bias)
                acc2[rows, :] = a
                l2[rows, :] = l
                m2[rows, :] = m
        return carry

    lax.fori_loop(0, d2 // classes_per_body, pat2, 0)

    def finish(rows, a1, l1, m1):
        a2_, l2_, m2_ = acc2[rows, :], l2[rows, :], m2[rows, :]
        a3_, l3_, m3_ = acc3[rows, :], l3[rows, :], m3[rows, :]
        mx = jnp.maximum(jnp.maximum(m1, m2_), m3_)
        w1, w2, w3 = jnp.exp2(m1 - mx), jnp.exp2(m2_ - mx), jnp.exp2(m3_ - mx)
        num = w1 * a1 + w2 * a2_ + w3 * a3_
        den = w1 * l1 + w2 * l2_ + w3 * l3_
        o_ref[rows, :] = (num / den).astype(o_ref.dtype)

    def pat1_block(start):
        rows = pl.ds(start, blk)
        kstart = start - blk if isinstance(start, int) else pl.multiple_of(start - blk, blk)
        krows = pl.ds(kstart, 2 * blk)
        finish(rows, *block(q_ref[rows, :], k_ref[krows, :], v_ref[krows, :], full_bias(0)))

    rows0 = pl.ds(0, blk)
    finish(rows0, *block(q_ref[rows0, :], k_ref[rows0, :], v_ref[rows0, :], cur_bias(0)))
    for u in range(1, UNROLL):
        pat1_block(u * blk)

    def pat1(g, carry):
        for u in range(UNROLL):
            pat1_block(pl.multiple_of((g * UNROLL + u) * blk, blk))
        return carry

    lax.fori_loop(1, seq // (blk * UNROLL), pat1, 0)


def _attn_prompt(q, k, v, bias_p, *, hd):
    b, s, da = q.shape
    n_pairs = da // LANES
    n_pat = bias_p.shape[0]
    blk = N_BACK
    tok = pl.BlockSpec((None, s, LANES), lambda i, j: (i, 0, j))
    return pl.pallas_call(
        functools.partial(_attn_prompt_body, hd=hd),
        grid=(b, n_pairs),
        in_specs=[tok, tok, tok,
                  pl.BlockSpec((n_pat, 2, blk, 2 * blk), lambda i, j: (0, j, 0, 0))],
        out_specs=tok,
        out_shape=jax.ShapeDtypeStruct((b, s, da), BF16),
        scratch_shapes=[pltpu.VMEM((s, LANES), F32) for _ in range(6)],
        compiler_params=_cparams("parallel", "arbitrary"),
        name="attn_prompt",
    )(q, k, v, bias_p)


def _ssd_prompt_body(gate_ref, xbc_ref, dt_ref, alog_ref, dskip_ref, gssm_ref, *rest,
                     n_heads, hd, n_groups, d_state, guest):
    n_in = len(guest.operands) if guest else 0
    guest_in, (y_ref, hout_ref), guest_out, ht_ref = rest[:n_in], rest[n_in:n_in + 2], rest[n_in + 2:-1], rest[-1]
    step = pl.program_id(1)
    ts = gate_ref.shape[0]
    q = SSD_CHUNK
    d_inner = n_heads * hd
    gw = d_inner // n_groups
    pairs_per_group = gw // LANES

    @pl.when(step == 0)
    def _():
        ht_ref[...] = jnp.zeros_like(ht_ref)

    lane = lax.broadcasted_iota(jnp.int32, (1, LANES), 1)
    head_mask = (lane < hd, lane >= hd)
    row_q = lax.broadcasted_iota(jnp.int32, (q, q), 0)
    col_q = lax.broadcasted_iota(jnp.int32, (q, q), 1)
    causal = row_q >= col_q
    tri = jnp.where(causal, 1.0, 0.0).astype(BF16)

    neg_a_log2e = -jnp.exp(alog_ref[...]) * LOG2E

    def chunk(c, carry):
        t0 = pl.multiple_of(c * q, q)
        rows = pl.ds(t0, q)
        xs = xbc_ref[rows, :d_inner]
        bm = xbc_ref[rows, d_inner:d_inner + n_groups * d_state]
        cm = xbc_ref[rows, d_inner + n_groups * d_state:]

        dt = dt_ref[rows, :]
        la = dt * neg_a_log2e
        la_hi = la.astype(BF16)
        r1 = la - la_hi.astype(F32)
        la_mid = r1.astype(BF16)
        la_lo = (r1 - la_mid.astype(F32)).astype(BF16)
        a_cs = _dot(tri, la_hi) + _dot(tri, la_mid) + _dot(tri, la_lo)
        total = a_cs[q - 1:q, :]
        w_s = dt * jnp.exp2(total - a_cs)
        chunk_decay = jnp.exp2(total)
        a_cs_t, dt_t, w_t = a_cs.T, dt.T, w_s.T

        y_tiles = []
        for g in range(n_groups):
            b_g = bm[:, g * d_state:(g + 1) * d_state]
            c_g = cm[:, g * d_state:(g + 1) * d_state]
            cb = _dot_nt(c_g.astype(BF16), b_g.astype(BF16))
            b_t = b_g.T
            for jl in range(pairs_per_group):
                j = g * pairs_per_group + jl
                pair = slice(jl * LANES, (jl + 1) * LANES)
                xs_pair = xs[:, j * LANES:(j + 1) * LANES]
                ht_pair = ht_ref[g, :, pair]
                y_pair = None
                st_pair = None
                cd_pair = None
                for h in range(2):
                    e = 2 * j + h
                    col = jnp.broadcast_to(a_cs[:, e:e + 1], (q, q))
                    lmat = jnp.exp2(jnp.where(causal, col - a_cs_t[e:e + 1, :], NEG_INF))
                    col_n = col if d_state == q else jnp.broadcast_to(a_cs[:, e:e + 1], (q, d_state))
                    lhs = jnp.concatenate([(cb * lmat * dt_t[e:e + 1, :]).astype(BF16),
                                           (c_g * jnp.exp2(col_n)).astype(BF16)], axis=1)
                    xs_h = jnp.where(head_mask[h], xs_pair, 0.0).astype(BF16)
                    rhs = jnp.concatenate([xs_h, jnp.where(head_mask[h], ht_pair, 0.0).astype(BF16)], axis=0)
                    yh = _dot(lhs, rhs)
                    sth = _dot((b_t * w_t[e:e + 1, :]).astype(BF16), xs_h)
                    cdh = chunk_decay[:, e:e + 1]
                    y_pair = yh if y_pair is None else y_pair + yh
                    st_pair = sth if st_pair is None else st_pair + sth
                    cd_pair = cdh if cd_pair is None else jnp.where(head_mask[0], cd_pair, cdh)
                y_tiles.append(y_pair + dskip_ref[:, j * LANES:(j + 1) * LANES] * xs_pair)
                ht_ref[g, :, pair] = ht_pair * cd_pair + st_pair

        y = jnp.concatenate(y_tiles, axis=1)
        yg = y * gate_ref[rows, :]
        parts = []
        for g in range(n_groups):
            part = yg[:, g * gw:(g + 1) * gw]
            parts.append(part * lax.rsqrt(jnp.mean(part * part, axis=-1, keepdims=True) + EPS))
        y_ref[rows, :] = (jnp.concatenate(parts, axis=1) * gssm_ref[...]).astype(y_ref.dtype)
        return carry

    lax.fori_loop(0, ts // q, chunk, 0, unroll=min(4, ts // q))
    if guest:
        guest.body(*guest_in, *guest_out)

    @pl.when(step == pl.num_programs(1) - 1)
    def _():
        for g in range(n_groups):
            hout_ref[g * gw:(g + 1) * gw, :] = ht_ref[g].T


def _ssd_prompt(gate, xbc, dt, alog, dskip, gssm, *, n_heads, hd, n_groups, d_state, ts, guest=None):
    b, s, d_inner = gate.shape
    d_xbc = xbc.shape[-1]
    gw = d_inner // n_groups
    tok = lambda w: pl.BlockSpec((None, ts, w), lambda i, j: (i, j, 0))
    return pl.pallas_call(
        functools.partial(_ssd_prompt_body, n_heads=n_heads, hd=hd, n_groups=n_groups, d_state=d_state,
                          guest=guest),
        grid=(b, s // ts),
        in_specs=[tok(d_inner), tok(d_xbc), tok(LANES),
                  _const_spec(alog.shape), _const_spec(dskip.shape), _const_spec(gssm.shape)]
        + (guest.in_specs if guest else []),
        out_specs=[tok(d_inner), pl.BlockSpec((None, d_inner, d_state), lambda i, j: (i, 0, 0))]
        + (guest.out_specs if guest else []),
        out_shape=[jax.ShapeDtypeStruct((b, s, d_inner), BF16),
                   jax.ShapeDtypeStruct((b, d_inner, d_state), F32)] + (guest.out_shape if guest else []),
        scratch_shapes=[pltpu.VMEM((n_groups, d_state, gw), F32)],
        compiler_params=_cparams("parallel", "arbitrary"),
        name="ssd_prompt",
    )(gate, xbc, dt, alog, dskip, gssm, *(guest.operands if guest else []))


def _memkv_body(m_ref, g_ref, wk_ref, wv_ref, k_ref, v_ref, kh_ref, vh_ref):
    mn = _rms(m_ref[...], g_ref[...]).astype(BF16)
    n_heads, dx = kh_ref.shape[1:]
    for w_ref, flat_ref, heads_ref in ((wk_ref, k_ref, kh_ref), (wv_ref, v_ref, vh_ref)):
        res = _dot(mn, w_ref[...])
        flat_ref[...] = res
        for h in range(n_heads):
            heads_ref[:, h, :] = res[:, h * dx:(h + 1) * dx]


def _memkv(mem, g, wk, wv, *, n_heads, tm):
    n, d = mem.shape
    dk = wk.shape[1]
    row = pl.BlockSpec((tm, dk), lambda i: (i, 0))
    heads = pl.BlockSpec((tm, n_heads, dk // n_heads), lambda i: (i, 0, 0))
    return pl.pallas_call(
        _memkv_body,
        grid=(n // tm,),
        in_specs=[pl.BlockSpec((tm, d), lambda i: (i, 0)), _const_spec((1, d)),
                  _const_spec(wk.shape), _const_spec(wv.shape)],
        out_specs=[row, row, heads, heads],
        out_shape=[jax.ShapeDtypeStruct((n, dk), F32)] * 2
        + [jax.ShapeDtypeStruct((n, n_heads, dk // n_heads), F32)] * 2,
        compiler_params=_cparams("parallel"),
        name="memkv",
    )(mem, g, wk, wv)


def _post_prompt_body(x_ref, oa_ref, ys_ref, mk_ref, mv_ref, woa_ref, wos_ref, gc_ref, wcq_ref, wco_ref,
                      o_ref, *, n_heads_x):
    x2 = x_ref[...] + _dot(oa_ref[...], woa_ref[...]) + _dot(ys_ref[...], wos_ref[...])
    dx = x2.shape[1] // n_heads_x
    hq = _rms(x2, gc_ref[...]).astype(BF16)
    qx = (_dot(hq, wcq_ref[...]) * (dx ** -0.5)).astype(BF16)
    mk = mk_ref[...].astype(BF16)
    mv = mv_ref[...].astype(BF16)
    outs = []
    for h in range(n_heads_x):
        sl = slice(h * dx, (h + 1) * dx)
        s = _dot_nt(qx[:, sl], mk[:, sl])
        p = jnp.exp(s - jnp.max(s, axis=-1, keepdims=True))
        l = jnp.sum(p, axis=-1, keepdims=True)
        outs.append(_dot(p.astype(BF16), mv[:, sl]) / l)
    oc = jnp.concatenate(outs, axis=1).astype(BF16)
    o_ref[...] = x2 + _dot(oc, wco_ref[...])


def _post_prompt(x1, oa, ys, mk, mv, woa, wos, gc, wcq, wco, *, n_heads_x, tm):
    b, s, d = x1.shape
    n_mem = mk.shape[1]
    tok = lambda w: pl.BlockSpec((None, tm, w), lambda i, j: (i, j, 0))
    mem = pl.BlockSpec((None, n_mem, d), lambda i, j: (i, 0, 0))
    return pl.pallas_call(
        functools.partial(_post_prompt_body, n_heads_x=n_heads_x),
        grid=(b, s // tm),
        in_specs=[tok(d), tok(oa.shape[-1]), tok(ys.shape[-1]), mem, mem,
                  _const_spec(woa.shape), _const_spec(wos.shape), _const_spec(gc.shape),
                  _const_spec(wcq.shape), _const_spec(wco.shape)],
        out_specs=tok(d),
        out_shape=jax.ShapeDtypeStruct((b, s, d), F32),
        compiler_params=_cparams("parallel", "arbitrary"),
        name="post_prompt",
    )(x1, oa, ys, mk, mv, woa, wos, gc, wcq, wco)


def _attn_sample_body(q_ref, kn_ref, vn_ref, kt_ref, vt_ref, bias_ref, cnt_ref, o_ref, *, hd):
    da, w = kt_ref.shape
    n_heads = da // hd
    row = lax.broadcasted_iota(jnp.int32, (n_heads, da), 0)
    col = lax.broadcasted_iota(jnp.int32, (n_heads, da), 1)
    hmask = (col >= row * hd) & (col < (row + 1) * hd)
    qbd16 = jnp.where(hmask, q_ref[...], 0.0).astype(BF16)
    s = _dot(qbd16, kt_ref[...].astype(BF16)) + bias_ref[:, :w]
    kn = kn_ref[...].astype(BF16).astype(F32)
    s_new = jnp.sum(qbd16.astype(F32) * kn, axis=-1, keepdims=True) + bias_ref[:, w:w + 1]
    m = jnp.maximum(jnp.max(s, axis=-1, keepdims=True), s_new)
    e = jnp.exp(s - m) * cnt_ref[:, :w]
    e_new = jnp.exp(s_new - m) * cnt_ref[:, w:w + 1]
    den = jnp.sum(e, axis=-1, keepdims=True) + e_new
    acc = _dot_nt(e.astype(BF16), vt_ref[...].astype(BF16)) + e_new * vn_ref[...]
    o_ref[...] = jnp.sum(jnp.where(hmask, acc / den, 0.0), axis=0, keepdims=True)


def _attn_sample_guest(q, kn, vn, kct, vct, bias_s, count, *, hd, seq_of):
    bs, _, da = q.shape
    w = kct.shape[-1]
    row = _seq_spec((1, da), seq_of)
    cache = _seq_spec((da, w), seq_of)
    return Guest(functools.partial(_attn_sample_body, hd=hd),
                 (q, kn, vn, kct, vct, bias_s, count),
                 [row, row, row, cache, cache, _const_spec(bias_s.shape), _const_spec(count.shape)],
                 [jax.ShapeDtypeStruct((bs, 1, da), F32)], [row])


def _ssd_sample_body(z_ref, xbc_ref, dt_ref, cc_ref, st_ref, cw_ref, cb_ref, dtb_ref, alog_ref, dskip_ref,
                     gssm_ref, y_ref, nconv_ref, nst_ref, *, n_groups):
    n_pairs = z_ref.shape[0]
    pairs_per_group = n_pairs // n_groups
    gw = pairs_per_group * LANES
    row8 = lax.broadcasted_iota(jnp.int32, (n_pairs, 1), 0)
    neg_a = -jnp.exp(alog_ref[...])
    pad = jnp.zeros((LANES - 2 * n_pairs, LANES), F32)

    cc = cc_ref[...]
    xr = xbc_ref[...]
    pre = xr * cw_ref[CONV_WIDTH - 1] + cb_ref[...]
    for k in range(CONV_WIDTH - 1):
        pre = pre + cc[k] * cw_ref[k]
        nconv_ref[k] = cc[k + 1] if k + 1 < CONV_WIDTH - 1 else xr
    act = _silu(pre)
    xs = act[:n_pairs]
    bm = act[n_pairs:n_pairs + n_groups]
    cm = act[n_pairs + n_groups:]
    dt = _softplus(dt_ref[...] + dtb_ref[...])
    d_a = jnp.exp(dt * neg_a)
    cols = jnp.concatenate([xs * dt, d_a, pad], axis=0).T
    y = jnp.zeros((n_pairs, LANES), F32)
    for j in range(n_pairs):
        g = j // pairs_per_group
        rows = pl.ds(j * LANES, LANES)
        h_new = cols[:, n_pairs + j:n_pairs + j + 1] * st_ref[rows, :] + cols[:, j:j + 1] * bm[g:g + 1, :]
        nst_ref[rows, :] = h_new
        c_sel = jnp.where(row8 == j, cm[g:g + 1, :], 0.0).astype(BF16)
        y = y + _dot_nt(c_sel, h_new.astype(BF16))
    y = y + dskip_ref[...] * xs
    yg = y * _silu(z_ref[...])
    ss = jnp.sum(yg * yg, axis=-1, keepdims=True)
    mean = jnp.zeros_like(ss)
    for g in range(n_groups):
        in_g = (row8 >= g * pairs_per_group) & (row8 < (g + 1) * pairs_per_group)
        tot = jnp.sum(jnp.where(in_g, ss, 0.0), axis=0, keepdims=True)
        mean = jnp.where(in_g, tot / gw, mean)
    y_ref[...] = yg * lax.rsqrt(mean + EPS) * gssm_ref[...]


def _ssd_sample_guest(z, xbc, dt, cc, st, cw, cb, dtb, alog, dskip, gssm, *, n_groups, seq_of):
    seq_block = lambda a: _seq_spec(a.shape[1:], seq_of)
    consts = (cw, cb, dtb, alog, dskip, gssm)
    return Guest(functools.partial(_ssd_sample_body, n_groups=n_groups),
                 (z, xbc, dt, cc, st) + consts,
                 [seq_block(a) for a in (z, xbc, dt, cc, st)] + [_const_spec(a.shape) for a in consts],
                 [jax.ShapeDtypeStruct(a.shape, F32) for a in (z, cc, st)],
                 [seq_block(a) for a in (z, cc, st)])


def _mix_out_sample_body(x_ref, oa_ref, ys_ref, woa_ref, wos_ref, gc_ref, wcq_ref, x2_ref, qx_ref, *, scale):
    x2 = x_ref[...] + _dot(oa_ref[...].astype(BF16), woa_ref[...]) + _dot(ys_ref[...].astype(BF16), wos_ref[...])
    x2_ref[...] = x2
    qx_ref[...] = _dot(_rms(x2, gc_ref[...]).astype(BF16), wcq_ref[...]) * scale


def _mix_out_sample(x1, oa, ys, woa, wos, gc, wcq, *, scale):
    vm = pl.BlockSpec(memory_space=pltpu.VMEM)
    return pl.pallas_call(
        functools.partial(_mix_out_sample_body, scale=scale),
        in_specs=[vm] * 7,
        out_specs=[vm, vm],
        out_shape=[jax.ShapeDtypeStruct(x1.shape, F32), jax.ShapeDtypeStruct((x1.shape[0], wcq.shape[1]), F32)],
        compiler_params=pltpu.CompilerParams(vmem_limit_bytes=VMEM_LIMIT_BYTES),
        name="mix_out_sample",
    )(x1, oa, ys, woa, wos, gc, wcq)


def _lane_tiles(a, inverse=False):
    if inverse:
        *lead, rows, _ = a.shape
        heads = inverse
        return a.reshape(*lead, rows // heads, heads, LANES).swapaxes(-3, -2).reshape(*lead, heads, -1)
    *lead, heads, dx = a.shape
    return a.reshape(*lead, heads, dx // LANES, LANES).swapaxes(-3, -2).reshape(*lead, -1, LANES)


def _cross_sample_body(q_ref, mk_ref, mv_ref, o_ref, *, n_heads):
    rows = q_ref.shape[1]
    for u in range(q_ref.shape[0]):
        part = jnp.sum(mk_ref[u] * q_ref[u][None], axis=-1, keepdims=True)
        s = part
        for c in range(1, rows // n_heads):
            s = s + pltpu.roll(part, c * n_heads, 1)
        p = jnp.exp(s - jnp.max(s, axis=0, keepdims=True))
        l = jnp.sum(p, axis=0)
        o_ref[u] = jnp.sum(p * mv_ref[u], axis=0) / l


def _cross_sample_guest(qx, mk, mv, *, per_step):
    bs, n_mem, nh, dx = mk.shape
    rows = nh * dx // LANES
    row = pl.BlockSpec((per_step, rows, LANES), lambda t: (t, 0, 0))
    mem = pl.BlockSpec((per_step, n_mem, rows, LANES), lambda t: (t, 0, 0, 0))
    return Guest(functools.partial(_cross_sample_body, n_heads=nh),
                 (_lane_tiles(qx), _lane_tiles(mk), _lane_tiles(mv)), [row, mem, mem],
                 [jax.ShapeDtypeStruct((bs, rows, LANES), F32)], [row])


def _cross_out_sample_body(x_ref, oc_ref, wco_ref, o_ref):
    o_ref[...] = x_ref[...] + _dot(oc_ref[...].astype(BF16), wco_ref[...])


def _cross_out_sample(x2, oc, wco):
    vm = pl.BlockSpec(memory_space=pltpu.VMEM)
    return pl.pallas_call(
        _cross_out_sample_body,
        in_specs=[vm] * 3,
        out_specs=vm,
        out_shape=jax.ShapeDtypeStruct(x2.shape, F32),
        name="cross_out_sample",
    )(x2, oc, wco)


def _tile(n, target):
    t = min(n, target)
    while n % t or (t % SUBLANES and t != n):
        t -= 1
    return t


def kernel(x_prompt, x_sample, cache_win_k, cache_win_v, cache_conv, state_ssm, cache_mem_k, cache_mem_v, mem_prompt, rel_bias, g_ffn1, w1_gate, w1_up, w1_down, g_mix, w_in, conv_w, conv_b, dt_bias, a_log, d_skip, g_ssm, w_out, g_mem, w_ck, w_cv, g_cross, w_cq, w_co, g_ffn2, w2_gate, w2_up, w2_down, g_final):
    depth = g_ffn1.shape[0]
    b, s, d = x_prompt.shape
    bs, dec_seq, _ = x_sample.shape
    assert dec_seq == 1, "sample path handles one new token per sequence"
    n_heads_a, hd_a = cache_win_k.shape[-2:]
    d_attn = n_heads_a * hd_a
    n_heads_b, hd_b, d_state = state_ssm.shape[-3:]
    d_inner = n_heads_b * hd_b
    d_xbc = conv_w.shape[-1]
    n_groups = (d_xbc - d_inner) // (2 * d_state)
    n_mem, n_heads_x, hd_x = cache_mem_k.shape[-3:]
    w_buf = cache_win_k.shape[2]
    assert hd_a * 2 == LANES and hd_b * 2 == LANES and d_state == LANES
    assert s % (DILATIONS[-1] * N_BACK) == 0 and s >= CONV_WIDTH - 1

    cuts = (d_attn, 2 * d_attn, 3 * d_attn, 3 * d_attn + d_inner, 3 * d_attn + d_inner + d_xbc)
    d_in = w_in.shape[-1]
    d_in_pad = cuts[-1] + LANES
    n_pairs = d_inner // LANES
    xbc_rows = d_xbc // LANES

    def row(a):
        return a.reshape(1, -1)

    def per_feature(a):
        return jnp.repeat(a, hd_b).reshape(1, d_inner)

    def head_lanes(a):
        return jnp.pad(a, (0, LANES - n_heads_b)).reshape(1, LANES)

    bias_p, bias_s, count_s = _bias_tables(rel_bias, w_buf)

    yp = x_prompt.reshape(b * s, d)
    ysm = x_sample.reshape(bs, d)
    tm_p = _tile(b * s, 1024)
    tm_s = _tile(bs, 512)
    ffn_steps = (b * s) // tm_p
    assert bs % ffn_steps == 0, "the FFN's guest takes a whole number of sample sequences per grid step"
    gfin = row(g_final)
    outs = [[] for _ in range(10)]
    for i in range(depth):
        bf = lambda a: a[i].astype(BF16)
        w1g, w1u, w1d = bf(w1_gate), bf(w1_up), bf(w1_down)
        w2g, w2u, w2d = bf(w2_gate), bf(w2_up), bf(w2_down)
        w_in_p = jnp.pad(w_in[i], ((0, 0), (0, d_in_pad - d_in))).astype(BF16)
        woa, wos = w_out[i, :d_attn].astype(BF16), w_out[i, d_attn:].astype(BF16)
        wck, wcv, wcq, wco = bf(w_ck), bf(w_cv), bf(w_cq), bf(w_co)
        last = i == depth - 1

        x1s = _ffn(ysm, row(g_ffn1[i]), w1g, w1u, w1d, gfin, final_norm=False, tm=tm_s)
        qs, ks, vs, zs, xbcs, dts = _inproj(x1s, row(g_mix[i]), w_in_p, cuts=cuts, q_scale=hd_a ** -0.5, tm=tm_s)
        r3 = lambda a: a.reshape(bs, 1, a.shape[-1])
        to_t = lambda c: c.transpose(0, 2, 3, 1).reshape(bs, d_attn, w_buf)
        ts_ssd = _tile(s, 512)
        ssd_steps = s // ts_ssd
        assert b * ssd_steps == bs, "guest kernels take one sample sequence per prompt grid step"
        in_ssd = lambda bi, si: bi * ssd_steps + si
        attn_s = _attn_sample_guest(r3(qs), r3(ks), r3(vs), to_t(cache_win_k[i]), to_t(cache_win_v[i]),
                                    bias_s, count_s, hd=hd_a, seq_of=in_ssd)
        tiles = lambda a: a.reshape(a.shape[:-1] + (a.shape[-1] // LANES, LANES))
        dt_feat = jnp.repeat(dts[:, :n_heads_b], hd_b, axis=-1)
        ssd_s = _ssd_sample_guest(
            tiles(zs), tiles(xbcs), tiles(dt_feat), tiles(cache_conv[i]),
            state_ssm[i].reshape(bs, d_inner, d_state),
            tiles(conv_w[i]), tiles(conv_b[i]), tiles(jnp.repeat(dt_bias[i], hd_b)),
            tiles(jnp.repeat(a_log[i], hd_b)), tiles(jnp.repeat(d_skip[i], hd_b)), tiles(g_ssm[i]),
            n_groups=n_groups, seq_of=in_ssd)

        x1 = _ffn(yp, row(g_ffn1[i]), w1g, w1u, w1d, gfin, final_norm=False, tm=tm_p)
        q, k, v, gate, xbc, dtp, kt, vt, xbc_tail = _inproj_prompt(
            x1, row(g_mix[i]), w_in_p, conv_w[i], row(conv_b[i]), head_lanes(dt_bias[i]),
            cuts=cuts, q_scale=hd_a ** -0.5 * LOG2E, tm=_tile(s, 512), seq=s)
        to_seq = lambda a: a.reshape(b, s, a.shape[-1])
        oa = _attn_prompt(to_seq(q), to_seq(k), to_seq(v), bias_p, hd=hd_a)
        y_ssm, h_last, oas, ys_t, nconv, nstate = _ssd_prompt(
            to_seq(gate), to_seq(xbc), to_seq(dtp), head_lanes(a_log[i]), per_feature(d_skip[i]), row(g_ssm[i]),
            n_heads=n_heads_b, hd=hd_b, n_groups=n_groups, d_state=d_state, ts=ts_ssd, guest=attn_s + ssd_s)
        mk, mv, mk_h, mv_h = _memkv(mem_prompt.reshape(b * n_mem, d), row(g_mem[i]), wck, wcv,
                                    n_heads=n_heads_x, tm=_tile(b * n_mem, 512))
        x3 = _post_prompt(to_seq(x1), oa, y_ssm, mk.reshape(b, n_mem, d), mv.reshape(b, n_mem, d),
                          woa, wos, row(g_cross[i]), wcq, wco, n_heads_x=n_heads_x, tm=_tile(s, 512))
        x2s, qxs = _mix_out_sample(x1s, oas.reshape(bs, d_attn), ys_t.reshape(bs, d_inner), woa, wos,
                                   row(g_cross[i]), wcq, scale=hd_x ** -0.5)
        cross_s = _cross_sample_guest(qxs.reshape(bs, n_heads_x, hd_x), cache_mem_k[i], cache_mem_v[i],
                                      per_step=bs // ffn_steps)
        yp, ocs = _ffn(x3.reshape(b * s, d), row(g_ffn2[i]), w2g, w2u, w2d, gfin, final_norm=last, tm=tm_p,
                       guest=cross_s)
        keep = min(MAX_DISTANCE, s)
        from_t = lambda a: a.reshape(b, n_heads_a, hd_a, s).transpose(0, 3, 1, 2)[:, s - keep:]
        outs[0].append(from_t(kt))
        outs[1].append(from_t(vt))
        outs[2].append(xbc_tail[:, SUBLANES - (CONV_WIDTH - 1):])
        outs[3].append(h_last.reshape(b, n_heads_b, hd_b, d_state))
        outs[4].append(mk_h.reshape(b, n_mem, n_heads_x, hd_x))
        outs[5].append(mv_h.reshape(b, n_mem, n_heads_x, hd_x))

        x3s = _cross_out_sample(x2s, _lane_tiles(ocs, inverse=n_heads_x).reshape(bs, d), wco)
        ysm = _ffn(x3s, row(g_ffn2[i]), w2g, w2u, w2d, gfin, final_norm=last, tm=tm_s)
        outs[6].append(ks.reshape(bs, 1, n_heads_a, hd_a))
        outs[7].append(vs.reshape(bs, 1, n_heads_a, hd_a))
        outs[8].append(nconv.reshape(bs, CONV_WIDTH - 1, d_xbc))
        outs[9].append(nstate.reshape(bs, n_heads_b, hd_b, d_state))

    return (yp.reshape(b, s, d), ysm.reshape(bs, 1, d)) + tuple(jnp.stack(o) for o in outs)
```
